```python
import math
import jax, jax.numpy as jnp
from jax import lax
import numpy as np

D_MODEL = 1024
BATCH = 4
SEQ = 4096
DEPTH = 4

DA_HEADS = 4
DA_HEAD_DIM = 64
DA_Q_BLOCK = 128
SW_HEADS = 8
SW_KV_HEADS = 2
SW_HEAD_DIM = 64
SW_WINDOW = 128
D_FF = 2816
CONV_WIDTH = 3
EPS = 1e-6
NEG_INF = -1e30
N_BRANCHES = 2

DA_QK_WIDTH = DA_HEADS * 2 * DA_HEAD_DIM
DA_V_WIDTH = DA_HEADS * 2 * DA_HEAD_DIM
SW_Q_WIDTH = SW_HEADS * SW_HEAD_DIM
SW_KV_WIDTH = SW_KV_HEADS * SW_HEAD_DIM
IN_SPLITS = (DA_QK_WIDTH, DA_QK_WIDTH, DA_V_WIDTH, SW_Q_WIDTH, SW_KV_WIDTH, SW_KV_WIDTH, D_MODEL, D_MODEL)
IN_COLS = 3 * DA_QK_WIDTH + SW_Q_WIDTH + 2 * SW_KV_WIDTH + N_BRANCHES * D_MODEL

kernel_name = "hybrid_diffattn_swa_sinks_convffn"


def rmsnorm(x, w):
    xf = x.astype(jnp.float32)
    y = xf * lax.rsqrt(jnp.mean(xf * xf, axis=-1, keepdims=True) + EPS)
    return (y * w.astype(jnp.float32)).astype(x.dtype)


def alibi_slopes(n_heads):
    h = jnp.arange(1, n_heads + 1, dtype=jnp.float32)
    return jnp.exp2(-8.0 * h / n_heads)


def split_cols(proj):
    idx = []
    acc = 0
    for w in IN_SPLITS[:-1]:
        acc += w
        idx.append(acc)
    return jnp.split(proj, idx, axis=-1)


def diff_attention(q, k, v, lam, lam_init, subln_w):
    B, S, _ = q.shape
    H, d = DA_HEADS, DA_HEAD_DIM
    nb = S // DA_Q_BLOCK
    q = q.reshape(B, S, H, 2, d)
    k = k.reshape(B, S, H, 2, d)
    v = v.reshape(B, S, H, 2 * d)
    slopes = alibi_slopes(H)
    qb = jnp.moveaxis(q.reshape(B, nb, DA_Q_BLOCK, H, 2, d), 1, 0)
    key_pos = jnp.arange(S)
    scale = d ** -0.5

    def block(args):
        q_blk, i = args
        s = jnp.einsum('bqhcd,bshcd->bhcqs', q_blk, k, preferred_element_type=jnp.float32) * scale
        q_pos = i * DA_Q_BLOCK + jnp.arange(DA_Q_BLOCK)
        dist = q_pos[:, None] - key_pos[None, :]
        logits = s - slopes[:, None, None, None] * dist.astype(jnp.float32)
        logits = jnp.where(dist >= 0, logits, NEG_INF)
        p = jax.nn.softmax(logits, axis=-1)
        a = p[:, :, 0] - lam * p[:, :, 1]
        return jnp.einsum('bhqs,bshe->bqhe', a.astype(v.dtype), v)

    o = lax.map(block, (qb, jnp.arange(nb)))
    o = jnp.moveaxis(o, 0, 1).reshape(B, S, H, 2 * d)
    o = rmsnorm(o, subln_w) * (1.0 - lam_init)
    return o.reshape(B, S, H * 2 * d)


def sliding_window_attention(q, k, v, sinks):
    B, S, _ = q.shape
    W, KVH, d = SW_WINDOW, SW_KV_HEADS, SW_HEAD_DIM
    G = SW_HEADS // KVH
    nb = S // W
    qb = q.reshape(B, nb, W, KVH, G, d)

    def windows(t):
        tb = t.reshape(B, nb, W, KVH, d)
        tpad = jnp.concatenate([jnp.zeros_like(tb[:, :1]), tb], axis=1)
        return jnp.concatenate([tpad[:, :-1], tpad[:, 1:]], axis=2)

    kw, vw = windows(k), windows(v)
    s = jnp.einsum('bnqkgd,bnskd->bnkgqs', qb, kw, preferred_element_type=jnp.float32) * (d ** -0.5)
    q_loc = jnp.arange(W) + W
    k_loc = jnp.arange(2 * W)
    dist = q_loc[:, None] - k_loc[None, :]
    key_abs = jnp.arange(nb)[:, None] * W + k_loc[None, :] - W
    valid = ((dist >= 0) & (dist < W))[None, :, :] & (key_abs >= 0)[:, None, :]
    slopes = alibi_slopes(SW_HEADS).reshape(KVH, G)
    logits = s - slopes[:, :, None, None] * dist.astype(jnp.float32)
    logits = jnp.where(valid[None, :, None, None], logits, NEG_INF)
    sink = jnp.broadcast_to(sinks.astype(jnp.float32).reshape(KVH, G, 1, 1), logits.shape[:-1] + (1,))
    p = jax.nn.softmax(jnp.concatenate([logits, sink], axis=-1), axis=-1)[..., :-1]
    o = jnp.einsum('bnkgqs,bnskd->bnqkgd', p.astype(v.dtype), vw)
    return o.reshape(B, S, KVH * G * d)


def conv_ffn(xn, w_up, conv_w, conv_b, w_down):
    S = xn.shape[1]
    u = xn @ w_up
    upad = jnp.pad(u, ((0, 0), (CONV_WIDTH - 1, 0), (0, 0)))
    c = conv_b
    for j in range(CONV_WIDTH):
        c = c + upad[:, j:j + S] * conv_w[j]
    gate, val = jnp.split(c, 2, axis=-1)
    return (jax.nn.silu(gate) * val) @ w_down


def setup_inputs(seed: int = 0) -> dict:
    key = jax.random.key(seed)
    ks = jax.random.split(key, 18)
    f32 = jnp.float32
    nrm = lambda k, shape, s: jax.random.normal(k, shape, f32) * s
    return {
        "x": nrm(ks[0], (BATCH, SEQ, D_MODEL), 1.0),
        "norm_mix_w": 1.0 + nrm(ks[1], (DEPTH, D_MODEL), 0.02),
        "w_in": nrm(ks[2], (DEPTH, D_MODEL, IN_COLS), D_MODEL ** -0.5),
        "lambda_q1": nrm(ks[3], (DEPTH, DA_HEAD_DIM), 0.1),
        "lambda_k1": nrm(ks[4], (DEPTH, DA_HEAD_DIM), 0.1),
        "lambda_q2": nrm(ks[5], (DEPTH, DA_HEAD_DIM), 0.1),
        "lambda_k2": nrm(ks[6], (DEPTH, DA_HEAD_DIM), 0.1),
        "subln_w": 1.0 + nrm(ks[7], (DEPTH, 2 * DA_HEAD_DIM), 0.02),
        "sinks": nrm(ks[8], (DEPTH, SW_HEADS), 0.5),
        "w_br_da": nrm(ks[9], (DEPTH, DA_V_WIDTH, D_MODEL), DA_V_WIDTH ** -0.5),
        "w_br_sw": nrm(ks[10], (DEPTH, SW_Q_WIDTH, D_MODEL), SW_Q_WIDTH ** -0.5),
        "w_mix_out": nrm(ks[11], (DEPTH, D_MODEL, D_MODEL), D_MODEL ** -0.5),
        "norm_ffn_w": 1.0 + nrm(ks[12], (DEPTH, D_MODEL), 0.02),
        "w_up": nrm(ks[13], (DEPTH, D_MODEL, 2 * D_FF), D_MODEL ** -0.5),
        "conv_w": nrm(ks[14], (DEPTH, CONV_WIDTH, 2 * D_FF), CONV_WIDTH ** -0.5),
        "conv_b": nrm(ks[15], (DEPTH, 2 * D_FF), 0.02),
        "w_down": nrm(ks[16], (DEPTH, D_FF, D_MODEL), D_FF ** -0.5),
        "norm_final_w": 1.0 + nrm(ks[17], (D_MODEL,), 0.02),
    }


def reference(x, norm_mix_w, w_in, lambda_q1, lambda_k1, lambda_q2, lambda_k2, subln_w, sinks,
              w_br_da, w_br_sw, w_mix_out, norm_ffn_w, w_up, conv_w, conv_b, w_down, norm_final_w):
    for l in range(DEPTH):
        lam_init = 0.8 - 0.6 * math.exp(-0.3 * l)
        xn = rmsnorm(x, norm_mix_w[l])
        proj = xn @ w_in[l]
        da_q, da_k, da_v, sw_q, sw_k, sw_v, g_da, g_sw = split_cols(proj)
        lam = (jnp.exp(jnp.sum(lambda_q1[l].astype(jnp.float32) * lambda_k1[l].astype(jnp.float32)))
               - jnp.exp(jnp.sum(lambda_q2[l].astype(jnp.float32) * lambda_k2[l].astype(jnp.float32)))
               + lam_init)
        y_da = diff_attention(da_q, da_k, da_v, lam, lam_init, subln_w[l])
        y_sw = sliding_window_attention(sw_q, sw_k, sw_v, sinks[l])
        merged = jax.nn.sigmoid(g_da) * (y_da @ w_br_da[l]) + jax.nn.sigmoid(g_sw) * (y_sw @ w_br_sw[l])
        x = x + merged @ w_mix_out[l]
        xn = rmsnorm(x, norm_ffn_w[l])
        x = x + conv_ffn(xn, w_up[l], conv_w[l], conv_b[l], w_down[l])
    return rmsnorm(x, norm_final_w)
```

```python
import functools
import math

import jax
import jax.numpy as jnp
from jax import lax
from jax.experimental import pallas as pl
from jax.experimental.pallas import tpu as pltpu

D_MODEL = 1024
BATCH = 4
SEQ = 4096
DEPTH = 4
ROWS = BATCH * SEQ

DA_HEADS = 4
DA_HEAD_DIM = 64
SW_HEADS = 8
SW_KV_HEADS = 2
SW_HEAD_DIM = 64
SW_GROUP = SW_HEADS // SW_KV_HEADS
SW_WINDOW = 128
D_FF = 2816
CONV_WIDTH = 3
EPS = 1e-6
NEG_INF = -1e30

ATTN_COLS = 2304
GATE_COLS = 2 * D_MODEL
IN_COLS = ATTN_COLS + GATE_COLS
LANES = 128
DA_K_BLK = 4
DA_V_BLK = 8
SW_Q_BLK512 = 3
SW_K_BLK = 16
SW_V_BLK = 17

VMEM_LIMIT = 56 * 1024 * 1024

BF16 = jnp.bfloat16
F32 = jnp.float32

TM_PROJ = 512
PROJ_CH = 256
TQ_DA = 512
TQ_SW = 512
TM_MERGE = 512
TM_FFN = 512
FF_CH = 256
N_FF_CH = D_FF // FF_CH


def _rms(x, w):
    ms = jnp.mean(x * x, axis=-1, keepdims=True)
    return (x * lax.rsqrt(ms + EPS)) * w


def _inproj_kernel(x_ref, nw_ref, w_ref, attn_ref, gate_ref):
    xn = _rms(x_ref[...], nw_ref[...]).astype(BF16)
    for c in range(ATTN_COLS // PROJ_CH):
        sl = slice(c * PROJ_CH, (c + 1) * PROJ_CH)
        attn_ref[:, sl] = jnp.dot(xn, w_ref[:, sl], preferred_element_type=F32).astype(BF16)
    for c in range(GATE_COLS // PROJ_CH):
        src = slice(ATTN_COLS + c * PROJ_CH, ATTN_COLS + (c + 1) * PROJ_CH)
        dst = slice(c * PROJ_CH, (c + 1) * PROJ_CH)
        gate_ref[:, dst] = jnp.dot(xn, w_ref[:, src], preferred_element_type=F32).astype(BF16)


def _inproj(x, nw, w):
    return pl.pallas_call(
        _inproj_kernel,
        grid=(ROWS // TM_PROJ,),
        in_specs=[
            pl.BlockSpec((TM_PROJ, D_MODEL), lambda i: (i, 0)),
            pl.BlockSpec((1, D_MODEL), lambda i: (0, 0)),
            pl.BlockSpec((D_MODEL, IN_COLS), lambda i: (0, 0)),
        ],
        out_specs=[
            pl.BlockSpec((TM_PROJ, ATTN_COLS), lambda i: (i, 0)),
            pl.BlockSpec((TM_PROJ, GATE_COLS), lambda i: (i, 0)),
        ],
        out_shape=[
            jax.ShapeDtypeStruct((ROWS, ATTN_COLS), BF16),
            jax.ShapeDtypeStruct((ROWS, GATE_COLS), BF16),
        ],
        compiler_params=pltpu.CompilerParams(
            dimension_semantics=("arbitrary",), vmem_limit_bytes=VMEM_LIMIT),
        name="inproj",
    )(x, nw, w)


def _da_kernel(scal_ref, lq1_ref, lk1_ref, lq2_ref, lk2_ref, subw_ref, q_ref, k_ref, v_ref,
               o_ref, bias_ref, qs_ref, acc_ref, m_ref, l_ref):
    h = pl.program_id(1)
    i = pl.program_id(2)
    tq = TQ_DA
    slope = jnp.full((1, 1), scal_ref[h], F32)

    @pl.when(i == 0)
    def _():
        kk = lax.broadcasted_iota(jnp.int32, (tq, 2 * tq), 0)
        qq = lax.broadcasted_iota(jnp.int32, (tq, 2 * tq), 1)
        qq = jnp.where(qq >= tq, qq - tq, qq)
        dist = (qq - kk).astype(F32)
        rel = -slope * dist
        bias_ref[0] = rel
        bias_ref[1] = jnp.where(dist >= 0, rel, NEG_INF)

    q = q_ref[...] * jnp.asarray(DA_HEAD_DIM ** -0.5, BF16)
    lane = lax.broadcasted_iota(jnp.int32, (tq, LANES), 1)
    zero = jnp.zeros_like(q)
    qs_ref[:tq, :] = jnp.where(lane < DA_HEAD_DIM, q, zero)
    qs_ref[tq:, :] = jnp.where(lane >= DA_HEAD_DIM, q, zero)

    m_ref[...] = jnp.full(m_ref.shape, NEG_INF, F32)
    l_ref[...] = jnp.zeros(l_ref.shape, F32)
    acc_ref[...] = jnp.zeros(acc_ref.shape, F32)

    def step(j, bias, tile_shift):
        start = pl.multiple_of(j * tq, tq)
        k = k_ref[pl.ds(start, tq), :]
        v = v_ref[pl.ds(start, tq), :]
        s = lax.dot_general(k, qs_ref[...], (((1,), (1,)), ((), ())),
                            preferred_element_type=F32)
        u = s + bias
        m_old = m_ref[...]
        m_new = jnp.maximum(m_old, jnp.max(u, axis=0, keepdims=True) - tile_shift)
        alpha = jnp.exp(m_old - m_new)
        p = jnp.exp(u - (m_new + tile_shift))
        l_ref[...] = alpha * l_ref[...] + jnp.sum(p, axis=0, keepdims=True)
        pv = lax.dot_general(v, p.astype(BF16), (((0,), (0,)), ((), ())),
                             preferred_element_type=F32)
        acc_ref[...] = alpha * acc_ref[...] + pv
        m_ref[...] = m_new

    def body(j, carry):
        shift = slope * jnp.full((1, 1), (i - j) * tq, jnp.int32).astype(F32)
        step(j, bias_ref[0], shift)
        return carry

    lax.fori_loop(0, i, body, 0)
    step(i, bias_ref[1], jnp.zeros((1, 1), F32))

    lam_init = jnp.full((1, 1), scal_ref[DA_HEADS], F32)
    lam = (jnp.exp(jnp.sum(lq1_ref[...] * lk1_ref[...], axis=-1, keepdims=True))
           - jnp.exp(jnp.sum(lq2_ref[...] * lk2_ref[...], axis=-1, keepdims=True)) + lam_init)
    inv = 1.0 / l_ref[...]
    acc = acc_ref[...]
    o = acc[:, :tq] * inv[:, :tq] - lam * (acc[:, tq:] * inv[:, tq:])
    ms = jnp.mean(o * o, axis=0, keepdims=True)
    y = ((o * lax.rsqrt(ms + EPS)) * subw_ref[...]) * (1.0 - lam_init)
    o_ref[...] = y.T.astype(BF16)


def _da_attention(attn, scal, lq1, lk1, lq2, lk2, subw):
    nq = SEQ // TQ_DA
    vec = pl.BlockSpec((1, DA_HEAD_DIM), lambda b, h, i: (0, 0))
    return pl.pallas_call(
        _da_kernel,
        grid=(BATCH, DA_HEADS, nq),
        in_specs=[
            pl.BlockSpec(memory_space=pltpu.SMEM),
            vec, vec, vec, vec,
            pl.BlockSpec((2 * DA_HEAD_DIM, 1), lambda b, h, i: (0, 0)),
            pl.BlockSpec((TQ_DA, LANES), lambda b, h, i: (b * nq + i, h)),
            pl.BlockSpec((SEQ, LANES), lambda b, h, i: (b, DA_K_BLK + h)),
            pl.BlockSpec((SEQ, LANES), lambda b, h, i: (b, DA_V_BLK + h)),
        ],
        out_specs=pl.BlockSpec((TQ_DA, LANES), lambda b, h, i: (b * nq + i, h)),
        out_shape=jax.ShapeDtypeStruct((ROWS, DA_HEADS * 2 * DA_HEAD_DIM), BF16),
        scratch_shapes=[
            pltpu.VMEM((2, TQ_DA, 2 * TQ_DA), F32),
            pltpu.VMEM((2 * TQ_DA, LANES), BF16),
            pltpu.VMEM((2 * DA_HEAD_DIM, 2 * TQ_DA), F32),
            pltpu.VMEM((1, 2 * TQ_DA), F32),
            pltpu.VMEM((1, 2 * TQ_DA), F32),
        ],
        compiler_params=pltpu.CompilerParams(
            dimension_semantics=("arbitrary", "arbitrary", "arbitrary"),
            vmem_limit_bytes=VMEM_LIMIT),
        name="diff_attn",
    )(scal, lq1, lk1, lq2, lk2, subw, attn, attn, attn)


def _sw_kernel(slope_ref, sink_ref, q_ref, k_ref, v_ref, o_ref, tab_ref):
    b = pl.program_id(0)
    i = pl.program_id(1)
    w = SW_WINDOW
    d = SW_HEAD_DIM
    ncol = SW_GROUP * w

    @pl.when((b == 0) & (i == 0))
    def _():
        kk = lax.broadcasted_iota(jnp.int32, (2 * w, ncol), 0)
        col = lax.broadcasted_iota(jnp.int32, (2 * w, ncol), 1)
        qq = col & (w - 1)
        grp = col >> 7
        for c in range(SW_KV_HEADS):
            slope = jnp.zeros((2 * w, ncol), F32)
            for g in range(SW_GROUP):
                slope = jnp.where(grp == g, slope_ref[c * SW_GROUP + g], slope)
            dist = qq + w - kk
            tab_ref[c, 0] = jnp.where((dist >= 0) & (dist < w), -slope * dist.astype(F32), NEG_INF)
            dist0 = qq - kk
            tab_ref[c, 1] = jnp.where(dist0 >= 0, -slope * dist0.astype(F32), NEG_INF)

    lane = lax.broadcasted_iota(jnp.int32, (w, LANES), 1)
    n_blk = TQ_SW // w
    for bb in range(n_blk):
        n = i * n_blk + bb
        if bb == 0:
            start = pl.multiple_of(jnp.maximum(n - 1, 0) * w, w)
            tsel = jnp.where(n == 0, 1, 0)
        else:
            start = pl.multiple_of((n - 1) * w, w)
            tsel = 0
        kwin = k_ref[pl.ds(start, 2 * w), :]
        vwin = v_ref[pl.ds(start, 2 * w), :]
        kswp = jnp.concatenate([kwin[:, d:], kwin[:, :d]], axis=1)
        qblk = q_ref[bb * w:(bb + 1) * w, :] * jnp.asarray(d ** -0.5, BF16)
        o_rows = []
        for c in range(SW_KV_HEADS):
            s_parts = []
            for g in range(SW_GROUP):
                hh = c * SW_GROUP + g
                half = hh % 2
                qpair = qblk[:, (hh // 2) * LANES:(hh // 2 + 1) * LANES]
                keep = (lane >= d) if half else (lane < d)
                qz = jnp.where(keep, qpair, jnp.zeros_like(qpair))
                kk_ = kwin if half == c else kswp
                s_parts.append(lax.dot_general(kk_, qz, (((1,), (1,)), ((), ())),
                                               preferred_element_type=F32))
            u = jnp.concatenate(s_parts, axis=1) + tab_ref[c, tsel]
            sink = sink_ref[c]
            m = jnp.maximum(jnp.max(u, axis=0, keepdims=True), sink)
            p = jnp.exp(u - m)
            inv = 1.0 / (jnp.sum(p, axis=0, keepdims=True) + jnp.exp(sink - m))
            pb = p.astype(BF16)
            for g in range(SW_GROUP):
                ot = lax.dot_general(vwin, pb[:, g * w:(g + 1) * w], (((0,), (0,)), ((), ())),
                                     preferred_element_type=F32)
                o_rows.append(ot[c * d:(c + 1) * d, :] * inv[:, g * w:(g + 1) * w])
        o_t = jnp.concatenate(o_rows, axis=0)
        o_ref[bb * w:(bb + 1) * w, :] = o_t.T.astype(BF16)


def _sw_attention(attn, slopes, sink_rows):
    nq = SEQ // TQ_SW
    width = SW_HEADS * SW_HEAD_DIM
    return pl.pallas_call(
        _sw_kernel,
        grid=(BATCH, nq),
        in_specs=[
            pl.BlockSpec(memory_space=pltpu.SMEM),
            pl.BlockSpec((SW_KV_HEADS, 1, SW_GROUP * SW_WINDOW), lambda b, i: (0, 0, 0)),
            pl.BlockSpec((TQ_SW, width), lambda b, i: (b * nq + i, SW_Q_BLK512)),
            pl.BlockSpec((SEQ, LANES), lambda b, i: (b, SW_K_BLK)),
            pl.BlockSpec((SEQ, LANES), lambda b, i: (b, SW_V_BLK)),
        ],
        out_specs=pl.BlockSpec((TQ_SW, width), lambda b, i: (b * nq + i, 0)),
        out_shape=jax.ShapeDtypeStruct((ROWS, width), BF16),
        scratch_shapes=[
            pltpu.VMEM((SW_KV_HEADS, 2, 2 * SW_WINDOW, SW_GROUP * SW_WINDOW), F32),
        ],
        compiler_params=pltpu.CompilerParams(
            dimension_semantics=("arbitrary", "arbitrary"), vmem_limit_bytes=VMEM_LIMIT),
        name="sw_attn",
    )(slopes, sink_rows, attn, attn, attn)


def _sigmoid(x):
    return 1.0 / (1.0 + jnp.exp(-x))


def _merge_kernel(yda_ref, ysw_ref, g_ref, x_ref, wda_ref, wsw_ref, wmix_ref, nw_ref,
                  xo_ref, xn_ref):
    t_da = jnp.dot(yda_ref[...], wda_ref[...], preferred_element_type=F32)
    t_sw = jnp.dot(ysw_ref[...], wsw_ref[...], preferred_element_type=F32)
    g_da = g_ref[:, :D_MODEL].astype(F32)
    g_sw = g_ref[:, D_MODEL:].astype(F32)
    merged = _sigmoid(g_da) * t_da + _sigmoid(g_sw) * t_sw
    xo = x_ref[...] + jnp.dot(merged.astype(BF16), wmix_ref[...], preferred_element_type=F32)
    xo_ref[...] = xo
    xn_ref[...] = _rms(xo, nw_ref[...]).astype(BF16)


def _merge(yda, ysw, gates, x, wda, wsw, wmix, nw):
    half = DA_HEADS * 2 * DA_HEAD_DIM
    row = lambda i: (i, 0)
    fixed = lambda i: (0, 0)
    return pl.pallas_call(
        _merge_kernel,
        grid=(ROWS // TM_MERGE,),
        in_specs=[
            pl.BlockSpec((TM_MERGE, half), row),
            pl.BlockSpec((TM_MERGE, half), row),
            pl.BlockSpec((TM_MERGE, GATE_COLS), row),
            pl.BlockSpec((TM_MERGE, D_MODEL), row),
            pl.BlockSpec((half, D_MODEL), fixed),
            pl.BlockSpec((half, D_MODEL), fixed),
            pl.BlockSpec((D_MODEL, D_MODEL), fixed),
            pl.BlockSpec((1, D_MODEL), fixed),
        ],
        out_specs=[
            pl.BlockSpec((TM_MERGE, D_MODEL), row),
            pl.BlockSpec((TM_MERGE, D_MODEL), row),
        ],
        out_shape=[
            jax.ShapeDtypeStruct((ROWS, D_MODEL), F32),
            jax.ShapeDtypeStruct((ROWS, D_MODEL), BF16),
        ],
        compiler_params=pltpu.CompilerParams(
            dimension_semantics=("arbitrary",), vmem_limit_bytes=VMEM_LIMIT),
        name="merge",
    )(yda, ysw, gates, x, wda, wsw, wmix, nw)


def _ffn_kernel(xn_ref, x_ref, wup_ref, cw_ref, cb_ref, wdn_ref, o_ref,
                carry_ref, ubuf_ref, h_ref):
    i = pl.program_id(0)
    tm = TM_FFN
    seq_start = (i % (SEQ // tm)) == 0
    xn = xn_ref[...]

    def conv_chunk(slot, col0):
        sl = slice(col0, col0 + FF_CH)
        u = jnp.dot(xn, wup_ref[:, sl], preferred_element_type=F32)
        prev = jnp.where(seq_start, 0.0, carry_ref[:, sl])
        ubuf_ref[slot, 0:8, :] = prev
        ubuf_ref[slot, 8:, :] = u
        carry_ref[:, sl] = u[tm - 8:, :]
        return (cb_ref[:, sl]
                + ubuf_ref[slot, 6:6 + tm, :] * cw_ref[0:1, sl]
                + ubuf_ref[slot, 7:7 + tm, :] * cw_ref[1:2, sl]
                + u * cw_ref[2:3, sl])

    for c in range(N_FF_CH):
        gate = conv_chunk(0, c * FF_CH)
        val = conv_chunk(1, D_FF + c * FF_CH)
        h_ref[:, c * FF_CH:(c + 1) * FF_CH] = ((gate * _sigmoid(gate)) * val).astype(BF16)

    o_ref[...] = x_ref[...] + jnp.dot(h_ref[...], wdn_ref[...], preferred_element_type=F32)


def _ffn(xn, x, wup, cw, cb, wdn):
    row = lambda i: (i, 0)
    fixed = lambda i: (0, 0)
    return pl.pallas_call(
        _ffn_kernel,
        grid=(ROWS // TM_FFN,),
        in_specs=[
            pl.BlockSpec((TM_FFN, D_MODEL), row),
            pl.BlockSpec((TM_FFN, D_MODEL), row),
            pl.BlockSpec((D_MODEL, 2 * D_FF), fixed, pipeline_mode=pl.Buffered(1)),
            pl.BlockSpec((CONV_WIDTH, 2 * D_FF), fixed),
            pl.BlockSpec((1, 2 * D_FF), fixed),
            pl.BlockSpec((D_FF, D_MODEL), fixed, pipeline_mode=pl.Buffered(1)),
        ],
        out_specs=pl.BlockSpec((TM_FFN, D_MODEL), row),
        out_shape=jax.ShapeDtypeStruct((ROWS, D_MODEL), F32),
        scratch_shapes=[
            pltpu.VMEM((8, 2 * D_FF), F32),
            pltpu.VMEM((2, TM_FFN + 8, FF_CH), F32),
            pltpu.VMEM((TM_FFN, D_FF), BF16),
        ],
        compiler_params=pltpu.CompilerParams(
            dimension_semantics=("arbitrary",), vmem_limit_bytes=VMEM_LIMIT),
        name="conv_ffn",
    )(xn, x, wup, cw, cb, wdn)


def _final_norm_kernel(x_ref, nw_ref, o_ref):
    o_ref[...] = _rms(x_ref[...], nw_ref[...])


def _final_norm(x, nw):
    tm = 1024
    return pl.pallas_call(
        _final_norm_kernel,
        grid=(ROWS // tm,),
        in_specs=[pl.BlockSpec((tm, D_MODEL), lambda i: (i, 0)),
                  pl.BlockSpec((1, D_MODEL), lambda i: (0, 0))],
        out_specs=pl.BlockSpec((tm, D_MODEL), lambda i: (i, 0)),
        out_shape=jax.ShapeDtypeStruct((ROWS, D_MODEL), F32),
        compiler_params=pltpu.CompilerParams(
            dimension_semantics=("arbitrary",), vmem_limit_bytes=VMEM_LIMIT),
        name="final_norm",
    )(x, nw)


def _alibi_slopes(n_heads):
    hh = jnp.arange(1, n_heads + 1, dtype=F32)
    return jnp.exp2(-8.0 * hh / n_heads)


def kernel(x, norm_mix_w, w_in, lambda_q1, lambda_k1, lambda_q2, lambda_k2, subln_w, sinks,
           w_br_da, w_br_sw, w_mix_out, norm_ffn_w, w_up, conv_w, conv_b, w_down, norm_final_w):
    xf = x.reshape(ROWS, D_MODEL).astype(F32)
    w_in_b = w_in.astype(BF16)
    w_da_b = w_br_da.astype(BF16)
    w_sw_b = w_br_sw.astype(BF16)
    w_mix_b = w_mix_out.astype(BF16)
    w_up_b = w_up.astype(BF16)
    w_dn_b = w_down.astype(BF16)
    da_slopes = _alibi_slopes(DA_HEADS)
    sw_slopes = _alibi_slopes(SW_HEADS)

    for l in range(DEPTH):
        lam_init = 0.8 - 0.6 * math.exp(-0.3 * l)
        attn, gates = _inproj(xf, norm_mix_w[l].reshape(1, D_MODEL).astype(F32), w_in_b[l])
        da_scal = jnp.concatenate([da_slopes, jnp.full((1,), lam_init, F32)])
        vec = lambda a: a[l].reshape(1, DA_HEAD_DIM).astype(F32)
        y_da = _da_attention(attn, da_scal, vec(lambda_q1), vec(lambda_k1), vec(lambda_q2),
                             vec(lambda_k2), subln_w[l].reshape(2 * DA_HEAD_DIM, 1).astype(F32))
        sink_rows = jnp.repeat(sinks[l].astype(F32), SW_WINDOW).reshape(
            SW_KV_HEADS, 1, SW_GROUP * SW_WINDOW)
        y_sw = _sw_attention(attn, sw_slopes, sink_rows)
        xf, xn = _merge(y_da, y_sw, gates, xf, w_da_b[l], w_sw_b[l], w_mix_b[l],
                        norm_ffn_w[l].reshape(1, D_MODEL).astype(F32))
        xf = _ffn(xn, xf, w_up_b[l], conv_w[l].astype(F32),
                  conv_b[l].reshape(1, 2 * D_FF).astype(F32), w_dn_b[l])
    out = _final_norm(xf, norm_final_w.reshape(1, D_MODEL).astype(F32))
    return out.reshape(BATCH, SEQ, D_MODEL)
```

```python
import functools
import math

import jax
import jax.numpy as jnp
import numpy as np
from jax import lax
from jax.experimental import pallas as pl
from jax.experimental.pallas import tpu as pltpu

D_MODEL = 1024
BATCH = 4
SEQ = 4096
DEPTH = 4
ROWS = BATCH * SEQ

DA_HEADS = 4
DA_HEAD_DIM = 64
SW_HEADS = 8
SW_KV_HEADS = 2
SW_HEAD_DIM = 64
SW_GROUP = SW_HEADS // SW_KV_HEADS
SW_WINDOW = 128
D_FF = 2816
CONV_WIDTH = 3
EPS = 1e-6
NEG_INF = -1e30

ATTN_COLS = 2304
GATE_COLS = 2 * D_MODEL
IN_COLS = ATTN_COLS + GATE_COLS
LANES = 128
DA_K_BLK = 4
DA_V_BLK = 8
SW_Q_BLK512 = 3
SW_K_BLK = 16
SW_V_BLK = 17

VMEM_LIMIT = 56 * 1024 * 1024

BF16 = jnp.bfloat16
F32 = jnp.float32

TM_PROJ = 512
PROJ_CH = 256
TQ_DA = 512
TK_DA = 512
DA_VT_PAD = 16
DA_NT = 256
DA_POS_SHIFT = 8
TQ_SW = 512
TM_MERGE = 512
TM_FFN = 512
FF_CH = 256
N_FF_CH = D_FF // FF_CH


def _bf16_terms(x, n):
    terms, rest = [], float(x)
    for _ in range(n):
        t = float(np.asarray(rest, dtype=BF16).astype(np.float64))
        terms.append(t)
        rest -= t
    assert rest == 0.0, (x, terms)
    return terms


LOG2E = float(np.float32(1.4426950408889634))
LOG2E_TERMS = _bf16_terms(LOG2E, 3)
DA_Q_SCALE = LOG2E * DA_HEAD_DIM ** -0.5
DA_Q_COLS = DA_HEADS * 2 * DA_HEAD_DIM


def _rms(x, w):
    ms = jnp.mean(x * x, axis=-1, keepdims=True)
    return (x * lax.rsqrt(ms + EPS)) * w


def _inproj_kernel(x_ref, nw_ref, w_ref, attn_ref, gate_ref):
    xn = _rms(x_ref[...], nw_ref[...]).astype(BF16)
    for c in range(ATTN_COLS // PROJ_CH):
        sl = slice(c * PROJ_CH, (c + 1) * PROJ_CH)
        res = jnp.dot(xn, w_ref[:, sl], preferred_element_type=F32)
        if (c + 1) * PROJ_CH <= DA_Q_COLS:
            res = res * DA_Q_SCALE
        attn_ref[:, sl] = res.astype(BF16)
    for c in range(GATE_COLS // PROJ_CH):
        src = slice(ATTN_COLS + c * PROJ_CH, ATTN_COLS + (c + 1) * PROJ_CH)
        dst = slice(c * PROJ_CH, (c + 1) * PROJ_CH)
        gate_ref[:, dst] = jnp.dot(xn, w_ref[:, src], preferred_element_type=F32).astype(BF16)


def _inproj(x, nw, w):
    return pl.pallas_call(
        _inproj_kernel,
        grid=(ROWS // TM_PROJ,),
        in_specs=[
            pl.BlockSpec((TM_PROJ, D_MODEL), lambda i: (i, 0)),
            pl.BlockSpec((1, D_MODEL), lambda i: (0, 0)),
            pl.BlockSpec((D_MODEL, IN_COLS), lambda i: (0, 0)),
        ],
        out_specs=[
            pl.BlockSpec((TM_PROJ, ATTN_COLS), lambda i: (i, 0)),
            pl.BlockSpec((TM_PROJ, GATE_COLS), lambda i: (i, 0)),
        ],
        out_shape=[
            jax.ShapeDtypeStruct((ROWS, ATTN_COLS), BF16),
            jax.ShapeDtypeStruct((ROWS, GATE_COLS), BF16),
        ],
        compiler_params=pltpu.CompilerParams(
            dimension_semantics=("arbitrary",), vmem_limit_bytes=VMEM_LIMIT),
        name="inproj",
    )(x, nw, w)


def _da_kernel(scal_ref, lq1_ref, lk1_ref, lq2_ref, lk2_ref, subw_ref, q_ref, k_ref, v_ref,
               o_ref, pos_ref, mask_ref, vt_ref, qs_ref, sa_ref, sb_ref, acc_ref, m_ref, l_ref):
    b = pl.program_id(0)
    h = pl.program_id(1)
    i = pl.program_id(2)
    tq, tk, nt = TQ_DA, TK_DA, DA_NT
    tiles_per_map = tq // nt
    nt_dims = (((1,), (1,)), ((), ()))
    n_terms = len(LOG2E_TERMS)

    @pl.when(i == 0)
    def _():
        sub = lax.broadcasted_iota(jnp.int32, (DA_VT_PAD, tk), 0)
        ones_rows = jnp.where(sub == 0, 1.0, 0.0).astype(BF16)
        for jb in range(SEQ // tk):
            v_blk = v_ref[jb * tk:(jb + 1) * tk, :].astype(F32)
            vt_ref[jb, :2 * DA_HEAD_DIM, :] = v_blk.T.astype(BF16)
            vt_ref[jb, 2 * DA_HEAD_DIM:, :] = ones_rows

    @pl.when((b == 0) & (h == 0) & (i == 0))
    def _():
        row = lax.broadcasted_iota(jnp.int32, (SEQ, LANES), 0)
        lane = lax.broadcasted_iota(jnp.int32, (SEQ, LANES), 1)
        hi = (row >> DA_POS_SHIFT).astype(F32)
        lo = (row & ((1 << DA_POS_SHIFT) - 1)).astype(F32)
        pos = jnp.where(lane < 2 * n_terms, jnp.where((lane & 1) == 0, hi, lo), 0.0)
        pos_ref[...] = pos.astype(BF16)
        kk = lax.broadcasted_iota(jnp.int32, (nt, nt), 0)
        qq = lax.broadcasted_iota(jnp.int32, (nt, nt), 1)
        mask_ref[...] = jnp.where(kk <= qq, 0.0, NEG_INF)

    slope = jnp.full((1, 1), scal_ref[h], F32)
    q = q_ref[...]
    lane = lax.broadcasted_iota(jnp.int32, (tq, LANES), 1)
    zero = jnp.zeros_like(q)
    slope_blk = jnp.zeros((tq, LANES), F32)
    for t, term in enumerate(LOG2E_TERMS):
        slope_blk = jnp.where(lane == 2 * t, slope * (term * (1 << DA_POS_SHIFT)), slope_blk)
        slope_blk = jnp.where(lane == 2 * t + 1, slope * term, slope_blk)
    slope_blk = slope_blk.astype(BF16)
    qs_ref[:tq, :LANES] = jnp.where(lane < DA_HEAD_DIM, q, zero)
    qs_ref[tq:, :LANES] = jnp.where(lane >= DA_HEAD_DIM, q, zero)
    qs_ref[:tq, LANES:] = slope_blk
    qs_ref[tq:, LANES:] = slope_blk

    m_ref[...] = jnp.full(m_ref.shape, NEG_INF, F32)
    l_ref[...] = jnp.zeros(l_ref.shape, F32)
    acc_ref[...] = jnp.zeros(acc_ref.shape, F32)

    n_cols = 2 * tiles_per_map

    def scores(j, s_ref, tiles=range(n_cols)):
        start = pl.multiple_of(j * tk, tk)
        kx = jnp.concatenate([k_ref[pl.ds(start, tk), :], pos_ref[pl.ds(start, tk), :]], axis=1)
        for c in tiles:
            cols = slice(c * nt, (c + 1) * nt)
            s_ref[:, cols] = lax.dot_general(kx, qs_ref[cols, :], nt_dims,
                                             preferred_element_type=F32)

    def softmax_pv(j, s_ref, diagonal, tiles=range(n_cols)):
        for c in tiles:
            cols = slice(c * nt, (c + 1) * nt)
            n_sub = c % tiles_per_map + 1 if diagonal else tk // nt
            rows = n_sub * nt
            s = s_ref[:rows, cols]
            if diagonal and rows == nt:
                s = s + mask_ref[...]
            elif diagonal:
                s = jnp.concatenate([s[:rows - nt], s[rows - nt:] + mask_ref[...]], axis=0)
            m_old = m_ref[:, cols]
            m_new = jnp.maximum(m_old, jnp.max(s, axis=0, keepdims=True))
            alpha = jnp.exp2(m_old - m_new)
            p = jnp.exp2(s - m_new).astype(BF16)
            pv = jnp.dot(vt_ref[j, :, :rows], p, preferred_element_type=F32)
            acc_ref[:, cols] = alpha * acc_ref[:, cols] + pv[:2 * DA_HEAD_DIM]
            l_ref[:, cols] = alpha * l_ref[:, cols] + pv[2 * DA_HEAD_DIM:2 * DA_HEAD_DIM + 1]
            m_ref[:, cols] = m_new

    scores(0, sa_ref)

    def overlapped(j_next, next_ref, j_cur, cur_ref):
        for half in (range(n_cols // 2), range(n_cols // 2, n_cols)):
            scores(j_next, next_ref, half)
            softmax_pv(j_cur, cur_ref, False, half)

    def pair(t, carry):
        overlapped(2 * t + 1, sb_ref, 2 * t, sa_ref)
        overlapped(2 * t + 2, sa_ref, 2 * t + 1, sb_ref)
        return carry

    lax.fori_loop(0, i // 2, pair, 0)

    @pl.when(i % 2 == 1)
    def _():
        overlapped(i, sb_ref, i - 1, sa_ref)
        softmax_pv(i, sb_ref, True)

    @pl.when(i % 2 == 0)
    def _():
        softmax_pv(i, sa_ref, True)

    lam_init = jnp.full((1, 1), scal_ref[DA_HEADS], F32)
    lam = (jnp.exp(jnp.sum(lq1_ref[...] * lk1_ref[...], axis=-1, keepdims=True))
           - jnp.exp(jnp.sum(lq2_ref[...] * lk2_ref[...], axis=-1, keepdims=True)) + lam_init)
    inv = 1.0 / l_ref[...]
    acc = acc_ref[...]
    o = acc[:, :tq] * inv[:, :tq] - lam * (acc[:, tq:] * inv[:, tq:])
    ms = jnp.mean(o * o, axis=0, keepdims=True)
    y = ((o * lax.rsqrt(ms + EPS)) * subw_ref[...]) * (1.0 - lam_init)
    o_ref[...] = y.T.astype(BF16)


def _da_attention(attn, scal, lq1, lk1, lq2, lk2, subw):
    nq = SEQ // TQ_DA
    vec = pl.BlockSpec((1, DA_HEAD_DIM), lambda b, h, i: (0, 0))
    return pl.pallas_call(
        _da_kernel,
        grid=(BATCH, DA_HEADS, nq),
        in_specs=[
            pl.BlockSpec(memory_space=pltpu.SMEM),
            vec, vec, vec, vec,
            pl.BlockSpec((2 * DA_HEAD_DIM, 1), lambda b, h, i: (0, 0)),
            pl.BlockSpec((TQ_DA, LANES), lambda b, h, i: (b * nq + i, h)),
            pl.BlockSpec((SEQ, LANES), lambda b, h, i: (b, DA_K_BLK + h)),
            pl.BlockSpec((SEQ, LANES), lambda b, h, i: (b, DA_V_BLK + h)),
        ],
        out_specs=pl.BlockSpec((TQ_DA, LANES), lambda b, h, i: (b * nq + i, h)),
        out_shape=jax.ShapeDtypeStruct((ROWS, DA_HEADS * 2 * DA_HEAD_DIM), BF16),
        scratch_shapes=[
            pltpu.VMEM((SEQ, LANES), BF16),
            pltpu.VMEM((DA_NT, DA_NT), F32),
            pltpu.VMEM((SEQ // TK_DA, 2 * DA_HEAD_DIM + DA_VT_PAD, TK_DA), BF16),
            pltpu.VMEM((2 * TQ_DA, 2 * LANES), BF16),
            pltpu.VMEM((TK_DA, 2 * TQ_DA), F32),
            pltpu.VMEM((TK_DA, 2 * TQ_DA), F32),
            pltpu.VMEM((2 * DA_HEAD_DIM, 2 * TQ_DA), F32),
            pltpu.VMEM((1, 2 * TQ_DA), F32),
            pltpu.VMEM((1, 2 * TQ_DA), F32),
        ],
        compiler_params=pltpu.CompilerParams(
            dimension_semantics=("arbitrary", "arbitrary", "arbitrary"),
            vmem_limit_bytes=VMEM_LIMIT),
        name="diff_attn",
    )(scal, lq1, lk1, lq2, lk2, subw, attn, attn, attn)


def _sw_kernel(slope_ref, sink_ref, q_ref, k_ref, v_ref, o_ref, tab_ref):
    b = pl.program_id(0)
    i = pl.program_id(1)
    w = SW_WINDOW
    d = SW_HEAD_DIM
    ncol = SW_GROUP * w

    @pl.when((b == 0) & (i == 0))
    def _():
        kk = lax.broadcasted_iota(jnp.int32, (2 * w, ncol), 0)
        col = lax.broadcasted_iota(jnp.int32, (2 * w, ncol), 1)
        qq = col & (w - 1)
        grp = col >> 7
        for c in range(SW_KV_HEADS):
            slope = jnp.zeros((2 * w, ncol), F32)
            for g in range(SW_GROUP):
                slope = jnp.where(grp == g, slope_ref[c * SW_GROUP + g], slope)
            dist = qq + w - kk
            tab_ref[c, 0] = jnp.where((dist >= 0) & (dist < w), -slope * dist.astype(F32), NEG_INF)
            dist0 = qq - kk
            tab_ref[c, 1] = jnp.where(dist0 >= 0, -slope * dist0.astype(F32), NEG_INF)

    lane = lax.broadcasted_iota(jnp.int32, (w, LANES), 1)
    n_blk = TQ_SW // w
    for bb in range(n_blk):
        n = i * n_blk + bb
        if bb == 0:
            start = pl.multiple_of(jnp.maximum(n - 1, 0) * w, w)
            tsel = jnp.where(n == 0, 1, 0)
        else:
            start = pl.multiple_of((n - 1) * w, w)
            tsel = 0
        kwin = k_ref[pl.ds(start, 2 * w), :]
        vwin = v_ref[pl.ds(start, 2 * w), :]
        kswp = jnp.concatenate([kwin[:, d:], kwin[:, :d]], axis=1)
        qblk = q_ref[bb * w:(bb + 1) * w, :] * jnp.asarray(d ** -0.5, BF16)
        o_rows = []
        for c in range(SW_KV_HEADS):
            s_parts = []
            for g in range(SW_GROUP):
                hh = c * SW_GROUP + g
                half = hh % 2
                qpair = qblk[:, (hh // 2) * LANES:(hh // 2 + 1) * LANES]
                keep = (lane >= d) if half else (lane < d)
                qz = jnp.where(keep, qpair, jnp.zeros_like(qpair))
                kk_ = kwin if half == c else kswp
                s_parts.append(lax.dot_general(kk_, qz, (((1,), (1,)), ((), ())),
                                               preferred_element_type=F32))
            u = jnp.concatenate(s_parts, axis=1) + tab_ref[c, tsel]
            sink = sink_ref[c]
            m = jnp.maximum(jnp.max(u, axis=0, keepdims=True), sink)
            p = jnp.exp(u - m)
            inv = 1.0 / (jnp.sum(p, axis=0, keepdims=True) + jnp.exp(sink - m))
            pb = p.astype(BF16)
            for g in range(SW_GROUP):
                ot = lax.dot_general(vwin, pb[:, g * w:(g + 1) * w], (((0,), (0,)), ((), ())),
                                     preferred_element_type=F32)
                o_rows.append(ot[c * d:(c + 1) * d, :] * inv[:, g * w:(g + 1) * w])
        o_t = jnp.concatenate(o_rows, axis=0)
        o_ref[bb * w:(bb + 1) * w, :] = o_t.T.astype(BF16)


def _sw_attention(attn, slopes, sink_rows):
    nq = SEQ // TQ_SW
    width = SW_HEADS * SW_HEAD_DIM
    return pl.pallas_call(
        _sw_kernel,
        grid=(BATCH, nq),
        in_specs=[
            pl.BlockSpec(memory_space=pltpu.SMEM),
            pl.BlockSpec((SW_KV_HEADS, 1, SW_GROUP * SW_WINDOW), lambda b, i: (0, 0, 0)),
            pl.BlockSpec((TQ_SW, width), lambda b, i: (b * nq + i, SW_Q_BLK512)),
            pl.BlockSpec((SEQ, LANES), lambda b, i: (b, SW_K_BLK)),
            pl.BlockSpec((SEQ, LANES), lambda b, i: (b, SW_V_BLK)),
        ],
        out_specs=pl.BlockSpec((TQ_SW, width), lambda b, i: (b * nq + i, 0)),
        out_shape=jax.ShapeDtypeStruct((ROWS, width), BF16),
        scratch_shapes=[
            pltpu.VMEM((SW_KV_HEADS, 2, 2 * SW_WINDOW, SW_GROUP * SW_WINDOW), F32),
        ],
        compiler_params=pltpu.CompilerParams(
            dimension_semantics=("arbitrary", "arbitrary"), vmem_limit_bytes=VMEM_LIMIT),
        name="sw_attn",
    )(slopes, sink_rows, attn, attn, attn)


def _sigmoid(x):
    return 1.0 / (1.0 + jnp.exp(-x))


def _merge_kernel(yda_ref, ysw_ref, g_ref, x_ref, wda_ref, wsw_ref, wmix_ref, nw_ref,
                  xo_ref, xn_ref):
    t_da = jnp.dot(yda_ref[...], wda_ref[...], preferred_element_type=F32)
    t_sw = jnp.dot(ysw_ref[...], wsw_ref[...], preferred_element_type=F32)
    g_da = g_ref[:, :D_MODEL].astype(F32)
    g_sw = g_ref[:, D_MODEL:].astype(F32)
    merged = _sigmoid(g_da) * t_da + _sigmoid(g_sw) * t_sw
    xo = x_ref[...] + jnp.dot(merged.astype(BF16), wmix_ref[...], preferred_element_type=F32)
    xo_ref[...] = xo
    xn_ref[...] = _rms(xo, nw_ref[...]).astype(BF16)


def _merge(yda, ysw, gates, x, wda, wsw, wmix, nw):
    half = DA_HEADS * 2 * DA_HEAD_DIM
    row = lambda i: (i, 0)
    fixed = lambda i: (0, 0)
    return pl.pallas_call(
        _merge_kernel,
        grid=(ROWS // TM_MERGE,),
        in_specs=[
            pl.BlockSpec((TM_MERGE, half), row),
            pl.BlockSpec((TM_MERGE, half), row),
            pl.BlockSpec((TM_MERGE, GATE_COLS), row),
            pl.BlockSpec((TM_MERGE, D_MODEL), row),
            pl.BlockSpec((half, D_MODEL), fixed),
            pl.BlockSpec((half, D_MODEL), fixed),
            pl.BlockSpec((D_MODEL, D_MODEL), fixed),
            pl.BlockSpec((1, D_MODEL), fixed),
        ],
        out_specs=[
            pl.BlockSpec((TM_MERGE, D_MODEL), row),
            pl.BlockSpec((TM_MERGE, D_MODEL), row),
        ],
        out_shape=[
            jax.ShapeDtypeStruct((ROWS, D_MODEL), F32),
            jax.ShapeDtypeStruct((ROWS, D_MODEL), BF16),
        ],
        compiler_params=pltpu.CompilerParams(
            dimension_semantics=("arbitrary",), vmem_limit_bytes=VMEM_LIMIT),
        name="merge",
    )(yda, ysw, gates, x, wda, wsw, wmix, nw)


def _ffn_kernel(xn_ref, x_ref, wup_ref, cw_ref, cb_ref, wdn_ref, o_ref,
                carry_ref, ubuf_ref, h_ref):
    i = pl.program_id(0)
    tm = TM_FFN
    seq_start = (i % (SEQ // tm)) == 0
    xn = xn_ref[...]

    def conv_chunk(slot, col0):
        sl = slice(col0, col0 + FF_CH)
        u = jnp.dot(xn, wup_ref[:, sl], preferred_element_type=F32)
        prev = jnp.where(seq_start, 0.0, carry_ref[:, sl])
        ubuf_ref[slot, 0:8, :] = prev
        ubuf_ref[slot, 8:, :] = u
        carry_ref[:, sl] = u[tm - 8:, :]
        return (cb_ref[:, sl]
                + ubuf_ref[slot, 6:6 + tm, :] * cw_ref[0:1, sl]
                + ubuf_ref[slot, 7:7 + tm, :] * cw_ref[1:2, sl]
                + u * cw_ref[2:3, sl])

    for c in range(N_FF_CH):
        gate = conv_chunk(0, c * FF_CH)
        val = conv_chunk(1, D_FF + c * FF_CH)
        h_ref[:, c * FF_CH:(c + 1) * FF_CH] = ((gate * _sigmoid(gate)) * val).astype(BF16)

    o_ref[...] = x_ref[...] + jnp.dot(h_ref[...], wdn_ref[...], preferred_element_type=F32)


def _ffn(xn, x, wup, cw, cb, wdn):
    row = lambda i: (i, 0)
    fixed = lambda i: (0, 0)
    return pl.pallas_call(
        _ffn_kernel,
        grid=(ROWS // TM_FFN,),
        in_specs=[
            pl.BlockSpec((TM_FFN, D_MODEL), row),
            pl.BlockSpec((TM_FFN, D_MODEL), row),
            pl.BlockSpec((D_MODEL, 2 * D_FF), fixed, pipeline_mode=pl.Buffered(1)),
            pl.BlockSpec((CONV_WIDTH, 2 * D_FF), fixed),
            pl.BlockSpec((1, 2 * D_FF), fixed),
            pl.BlockSpec((D_FF, D_MODEL), fixed, pipeline_mode=pl.Buffered(1)),
        ],
        out_specs=pl.BlockSpec((TM_FFN, D_MODEL), row),
        out_shape=jax.ShapeDtypeStruct((ROWS, D_MODEL), F32),
        scratch_shapes=[
            pltpu.VMEM((8, 2 * D_FF), F32),
            pltpu.VMEM((2, TM_FFN + 8, FF_CH), F32),
            pltpu.VMEM((TM_FFN, D_FF), BF16),
        ],
        compiler_params=pltpu.CompilerParams(
            dimension_semantics=("arbitrary",), vmem_limit_bytes=VMEM_LIMIT),
        name="conv_ffn",
    )(xn, x, wup, cw, cb, wdn)


def _final_norm_kernel(x_ref, nw_ref, o_ref):
    o_ref[...] = _rms(x_ref[...], nw_ref[...])


def _final_norm(x, nw):
    tm = 1024
    return pl.pallas_call(
        _final_norm_kernel,
        grid=(ROWS // tm,),
        in_specs=[pl.BlockSpec((tm, D_MODEL), lambda i: (i, 0)),
                  pl.BlockSpec((1, D_MODEL), lambda i: (0, 0))],
        out_specs=pl.BlockSpec((tm, D_MODEL), lambda i: (i, 0)),
        out_shape=jax.ShapeDtypeStruct((ROWS, D_MODEL), F32),
        compiler_params=pltpu.CompilerParams(
            dimension_semantics=("arbitrary",), vmem_limit_bytes=VMEM_LIMIT),
        name="final_norm",
    )(x, nw)


def _alibi_slopes(n_heads):
    hh = jnp.arange(1, n_heads + 1, dtype=F32)
    return jnp.exp2(-8.0 * hh / n_heads)


def kernel(x, norm_mix_w, w_in, lambda_q1, lambda_k1, lambda_q2, lambda_k2, subln_w, sinks,
           w_br_da, w_br_sw, w_mix_out, norm_ffn_w, w_up, conv_w, conv_b, w_down, norm_final_w):
    xf = x.reshape(ROWS, D_MODEL).astype(F32)
    w_in_b = w_in.astype(BF16)
    w_da_b = w_br_da.astype(BF16)
    w_sw_b = w_br_sw.astype(BF16)
    w_mix_b = w_mix_out.astype(BF16)
    w_up_b = w_up.astype(BF16)
    w_dn_b = w_down.astype(BF16)
    da_slopes = _alibi_slopes(DA_HEADS)
    sw_slopes = _alibi_slopes(SW_HEADS)

    for l in range(DEPTH):
        lam_init = 0.8 - 0.6 * math.exp(-0.3 * l)
        attn, gates = _inproj(xf, norm_mix_w[l].reshape(1, D_MODEL).astype(F32), w_in_b[l])
        da_scal = jnp.concatenate([da_slopes, jnp.full((1,), lam_init, F32)])
        vec = lambda a: a[l].reshape(1, DA_HEAD_DIM).astype(F32)
        y_da = _da_attention(attn, da_scal, vec(lambda_q1), vec(lambda_k1), vec(lambda_q2),
                             vec(lambda_k2), subln_w[l].reshape(2 * DA_HEAD_DIM, 1).astype(F32))
        sink_rows = jnp.repeat(sinks[l].astype(F32), SW_WINDOW).reshape(
            SW_KV_HEADS, 1, SW_GROUP * SW_WINDOW)
        y_sw = _sw_attention(attn, sw_slopes, sink_rows)
        xf, xn = _merge(y_da, y_sw, gates, xf, w_da_b[l], w_sw_b[l], w_mix_b[l],
                        norm_ffn_w[l].reshape(1, D_MODEL).astype(F32))
        xf = _ffn(xn, xf, w_up_b[l], conv_w[l].astype(F32),
                  conv_b[l].reshape(1, 2 * D_FF).astype(F32), w_dn_b[l])
    out = _final_norm(xf, norm_final_w.reshape(1, D_MODEL).astype(F32))
    return out.reshape(BATCH, SEQ, D_MODEL)
```

```python
import functools
import math

import jax
import jax.numpy as jnp
import numpy as np
from jax import lax
from jax.experimental import pallas as pl
from jax.experimental.pallas import tpu as pltpu

D_MODEL = 1024
BATCH = 4
SEQ = 4096
DEPTH = 4
ROWS = BATCH * SEQ

DA_HEADS = 4
DA_HEAD_DIM = 64
SW_HEADS = 8
SW_KV_HEADS = 2
SW_HEAD_DIM = 64
SW_GROUP = SW_HEADS // SW_KV_HEADS
SW_WINDOW = 128
D_FF = 2816
CONV_WIDTH = 3
EPS = 1e-6
NEG_INF = -1e30

ATTN_COLS = 2304
GATE_COLS = 2 * D_MODEL
IN_COLS = ATTN_COLS + GATE_COLS
LANES = 128
SUBLANES = 8
DA_K_BLK = 4
DA_V_BLK = 8
SW_Q_BLK512 = 3
SW_K_BLK = 16
SW_V_BLK = 17

VMEM_LIMIT = 56 * 1024 * 1024

BF16 = jnp.bfloat16
F32 = jnp.float32

TM_PROJ = 512
PROJ_CH = 256
TQ_DA = 512
DA_TPS = 4
TK_DA = 512
DA_VT_PAD = 16
DA_NT = 256
DA_POS_SHIFT = 8
TQ_SW = 512
SW_COL_ORDER = (0, 2, 1, 3)
TM_MERGE = 512
TM_FFN = 512
FF_CH = 256
N_FF_CH = D_FF // FF_CH


def _bf16_terms(x, n):
    terms, rest = [], float(x)
    for _ in range(n):
        t = float(np.asarray(rest, dtype=BF16).astype(np.float64))
        terms.append(t)
        rest -= t
    assert rest == 0.0, (x, terms)
    return terms


LOG2E = float(np.float32(1.4426950408889634))
LOG2E_TERMS = _bf16_terms(LOG2E, 3)
DA_Q_SCALE = LOG2E * DA_HEAD_DIM ** -0.5
DA_Q_COLS = DA_HEADS * 2 * DA_HEAD_DIM


def _rms(x, w):
    ms = jnp.mean(x * x, axis=-1, keepdims=True)
    return (x * lax.rsqrt(ms + EPS)) * w


def _inproj_kernel(x_ref, nw_ref, w_ref, attn_ref, gate_ref):
    xn = _rms(x_ref[...], nw_ref[...]).astype(BF16)
    for c in range(ATTN_COLS // PROJ_CH):
        sl = slice(c * PROJ_CH, (c + 1) * PROJ_CH)
        res = jnp.dot(xn, w_ref[:, sl], preferred_element_type=F32)
        if (c + 1) * PROJ_CH <= DA_Q_COLS:
            res = res * DA_Q_SCALE
        attn_ref[:, sl] = res.astype(BF16)
    for c in range(GATE_COLS // PROJ_CH):
        src = slice(ATTN_COLS + c * PROJ_CH, ATTN_COLS + (c + 1) * PROJ_CH)
        dst = slice(c * PROJ_CH, (c + 1) * PROJ_CH)
        gate_ref[:, dst] = jnp.dot(xn, w_ref[:, src], preferred_element_type=F32).astype(BF16)


def _inproj(x, nw, w, layer):
    return pl.pallas_call(
        _inproj_kernel,
        grid=(ROWS // TM_PROJ,),
        in_specs=[
            pl.BlockSpec((TM_PROJ, D_MODEL), lambda i: (i, 0)),
            pl.BlockSpec((1, D_MODEL), lambda i: (0, 0)),
            pl.BlockSpec((None, D_MODEL, IN_COLS), lambda i: (layer, 0, 0)),
        ],
        out_specs=[
            pl.BlockSpec((TM_PROJ, ATTN_COLS), lambda i: (i, 0)),
            pl.BlockSpec((TM_PROJ, GATE_COLS), lambda i: (i, 0)),
        ],
        out_shape=[
            jax.ShapeDtypeStruct((ROWS, ATTN_COLS), BF16),
            jax.ShapeDtypeStruct((ROWS, GATE_COLS), BF16),
        ],
        compiler_params=pltpu.CompilerParams(
            dimension_semantics=("arbitrary",), vmem_limit_bytes=VMEM_LIMIT),
        name="inproj",
    )(x, nw, w)


def _da_kernel(scal_ref, lq1_ref, lk1_ref, lq2_ref, lk2_ref, subw_ref, q_ref, k_ref, v_ref,
               o_ref, pos_ref, mask_ref, vt_ref, qs_ref, sa_ref, sb_ref, acc_ref, m_ref, l_ref):
    b = pl.program_id(0)
    h = pl.program_id(1)
    g = pl.program_id(2)
    tq, tk, nt = TQ_DA, TK_DA, DA_NT
    tiles_per_map = tq // nt
    nt_dims = (((1,), (1,)), ((), ()))
    n_terms = len(LOG2E_TERMS)

    @pl.when(g == 0)
    def _():
        sub = lax.broadcasted_iota(jnp.int32, (DA_VT_PAD, tk), 0)
        ones_rows = jnp.where(sub == 0, 1.0, 0.0).astype(BF16)
        for jb in range(SEQ // tk):
            v_blk = v_ref[jb * tk:(jb + 1) * tk, :].astype(F32)
            vt_ref[jb, :2 * DA_HEAD_DIM, :] = v_blk.T.astype(BF16)
            vt_ref[jb, 2 * DA_HEAD_DIM:, :] = ones_rows

    @pl.when((b == 0) & (h == 0) & (g == 0))
    def _():
        row = lax.broadcasted_iota(jnp.int32, (SEQ, LANES), 0)
        lane = lax.broadcasted_iota(jnp.int32, (SEQ, LANES), 1)
        hi = (row >> DA_POS_SHIFT).astype(F32)
        lo = (row & ((1 << DA_POS_SHIFT) - 1)).astype(F32)
        pos = jnp.where(lane < 2 * n_terms, jnp.where((lane & 1) == 0, hi, lo), 0.0)
        pos_ref[...] = pos.astype(BF16)
        kk = lax.broadcasted_iota(jnp.int32, (nt, nt), 0)
        qq = lax.broadcasted_iota(jnp.int32, (nt, nt), 1)
        mask_ref[...] = jnp.where(kk <= qq, 0.0, NEG_INF)

    slope = jnp.full((1, 1), scal_ref[h], F32)
    lane = lax.broadcasted_iota(jnp.int32, (tq, LANES), 1)
    slope_blk = jnp.zeros((tq, LANES), F32)
    for t, term in enumerate(LOG2E_TERMS):
        slope_blk = jnp.where(lane == 2 * t, slope * (term * (1 << DA_POS_SHIFT)), slope_blk)
        slope_blk = jnp.where(lane == 2 * t + 1, slope * term, slope_blk)
    slope_blk = slope_blk.astype(BF16)
    for k in range(DA_TPS):
        q = q_ref[k * tq:(k + 1) * tq, :]
        zero = jnp.zeros_like(q)
        qs_ref[k, :tq, :LANES] = jnp.where(lane < DA_HEAD_DIM, q, zero)
        qs_ref[k, tq:, :LANES] = jnp.where(lane >= DA_HEAD_DIM, q, zero)
        qs_ref[k, :tq, LANES:] = slope_blk
        qs_ref[k, tq:, LANES:] = slope_blk

    def reset_state():
        m_ref[...] = jnp.full(m_ref.shape, NEG_INF, F32)
        l_ref[...] = jnp.zeros(l_ref.shape, F32)
        acc_ref[...] = jnp.zeros(acc_ref.shape, F32)

    n_cols = 2 * tiles_per_map
    halves = (range(n_cols // 2), range(n_cols // 2, n_cols))

    def scores(k, j, s_ref, tiles=range(n_cols)):
        start = pl.multiple_of(j * tk, tk)
        kx = jnp.concatenate([k_ref[pl.ds(start, tk), :], pos_ref[pl.ds(start, tk), :]], axis=1)
        for c in tiles:
            cols = slice(c * nt, (c + 1) * nt)
            s_ref[:, cols] = lax.dot_general(kx, qs_ref[k, cols, :], nt_dims,
                                             preferred_element_type=F32)

    def softmax_pv(j, s_ref, diagonal, tiles=range(n_cols)):
        for c in tiles:
            cols = slice(c * nt, (c + 1) * nt)
            n_sub = c % tiles_per_map + 1 if diagonal else tk // nt
            rows = n_sub * nt
            s = s_ref[:rows, cols]
            if diagonal and rows == nt:
                s = s + mask_ref[...]
            elif diagonal:
                s = jnp.concatenate([s[:rows - nt], s[rows - nt:] + mask_ref[...]], axis=0)
            m_old = m_ref[:, cols]
            m_new = jnp.maximum(m_old, jnp.max(s, axis=0, keepdims=True))
            alpha = jnp.exp2(m_old - m_new)
            p = jnp.exp2(s - m_new).astype(BF16)
            pv = jnp.dot(vt_ref[j, :, :rows], p, preferred_element_type=F32)
            acc_ref[:, cols] = alpha * acc_ref[:, cols] + pv[:2 * DA_HEAD_DIM]
            l_ref[:, cols] = alpha * l_ref[:, cols] + pv[2 * DA_HEAD_DIM:2 * DA_HEAD_DIM + 1]
            m_ref[:, cols] = m_new

    def overlapped(k_next, j_next, next_ref, j_cur, cur_ref, diagonal=False):
        for half in halves:
            scores(k_next, j_next, next_ref, half)
            softmax_pv(j_cur, cur_ref, diagonal, half)

    lam_init = jnp.full((1, 1), scal_ref[DA_HEADS], F32)
    lam = (jnp.exp(jnp.sum(lq1_ref[...] * lk1_ref[...], axis=-1, keepdims=True))
           - jnp.exp(jnp.sum(lq2_ref[...] * lk2_ref[...], axis=-1, keepdims=True)) + lam_init)

    def finish(k):
        inv = 1.0 / l_ref[...]
        acc = acc_ref[...]
        o = acc[:, :tq] * inv[:, :tq] - lam * (acc[:, tq:] * inv[:, tq:])
        ms = jnp.mean(o * o, axis=0, keepdims=True)
        y = ((o * lax.rsqrt(ms + EPS)) * subw_ref[...]) * (1.0 - lam_init)
        o_ref[k * tq:(k + 1) * tq, :] = y.T.astype(BF16)

    bufs = (sa_ref, sb_ref)
    cur = 0
    reset_state()
    scores(0, 0, bufs[cur])
    for k in range(DA_TPS):
        i = g * DA_TPS + k
        buf_a, buf_b = bufs[cur], bufs[1 - cur]

        def pair(t, carry, k=k, buf_a=buf_a, buf_b=buf_b):
            overlapped(k, 2 * t + 1, buf_b, 2 * t, buf_a)
            overlapped(k, 2 * t + 2, buf_a, 2 * t + 1, buf_b)
            return carry

        lax.fori_loop(0, g * (DA_TPS // 2) + k // 2, pair, 0)
        if k % 2 == 1:
            overlapped(k, i, buf_b, i - 1, buf_a)
            diag_buf, free_buf = buf_b, buf_a
        else:
            diag_buf, free_buf = buf_a, buf_b
        if k + 1 < DA_TPS:
            overlapped(k + 1, 0, free_buf, i, diag_buf, diagonal=True)
            finish(k)
            reset_state()
            cur = 0 if free_buf is bufs[0] else 1
        else:
            softmax_pv(i, diag_buf, True)
            finish(k)


def _da_attention(attn, scal, lq1, lk1, lq2, lk2, subw):
    rows = DA_TPS * TQ_DA
    nq = SEQ // rows
    vec = pl.BlockSpec((1, DA_HEAD_DIM), lambda b, h, i: (0, 0))
    return pl.pallas_call(
        _da_kernel,
        grid=(BATCH, DA_HEADS, nq),
        in_specs=[
            pl.BlockSpec(memory_space=pltpu.SMEM),
            vec, vec, vec, vec,
            pl.BlockSpec((2 * DA_HEAD_DIM, 1), lambda b, h, i: (0, 0)),
            pl.BlockSpec((rows, LANES), lambda b, h, i: (b * nq + i, h)),
            pl.BlockSpec((SEQ, LANES), lambda b, h, i: (b, DA_K_BLK + h)),
            pl.BlockSpec((SEQ, LANES), lambda b, h, i: (b, DA_V_BLK + h)),
        ],
        out_specs=pl.BlockSpec((rows, LANES), lambda b, h, i: (b * nq + i, h)),
        out_shape=jax.ShapeDtypeStruct((ROWS, DA_HEADS * 2 * DA_HEAD_DIM), BF16),
        scratch_shapes=[
            pltpu.VMEM((SEQ, LANES), BF16),
            pltpu.VMEM((DA_NT, DA_NT), F32),
            pltpu.VMEM((SEQ // TK_DA, 2 * DA_HEAD_DIM + DA_VT_PAD, TK_DA), BF16),
            pltpu.VMEM((DA_TPS, 2 * TQ_DA, 2 * LANES), BF16),
            pltpu.VMEM((TK_DA, 2 * TQ_DA), F32),
            pltpu.VMEM((TK_DA, 2 * TQ_DA), F32),
            pltpu.VMEM((2 * DA_HEAD_DIM, 2 * TQ_DA), F32),
            pltpu.VMEM((1, 2 * TQ_DA), F32),
            pltpu.VMEM((1, 2 * TQ_DA), F32),
        ],
        compiler_params=pltpu.CompilerParams(
            dimension_semantics=("arbitrary", "arbitrary", "arbitrary"),
            vmem_limit_bytes=VMEM_LIMIT),
        name="diff_attn",
    )(scal, lq1, lk1, lq2, lk2, subw, attn, attn, attn)


def _sw_kernel(slope_ref, sink_ref, q_ref, k_ref, v_ref, o_ref, tab_ref):
    b = pl.program_id(0)
    i = pl.program_id(1)
    w = SW_WINDOW
    d = SW_HEAD_DIM
    ncol = SW_GROUP * w

    @pl.when((b == 0) & (i == 0))
    def _():
        kk = lax.broadcasted_iota(jnp.int32, (2 * w, ncol), 0)
        col = lax.broadcasted_iota(jnp.int32, (2 * w, ncol), 1)
        qq = col & (w - 1)
        grp = col >> 7
        for c in range(SW_KV_HEADS):
            slope = jnp.zeros((2 * w, ncol), F32)
            for pos, g in enumerate(SW_COL_ORDER):
                slope = jnp.where(grp == pos, slope_ref[c * SW_GROUP + g], slope)
            dist = qq + w - kk
            tab_ref[c, 0] = jnp.where((dist >= 0) & (dist < w), -slope * dist.astype(F32), NEG_INF)
            dist0 = qq - kk
            tab_ref[c, 1] = jnp.where(dist0 >= 0, -slope * dist0.astype(F32), NEG_INF)

    lane = lax.broadcasted_iota(jnp.int32, (w, LANES), 1)
    n_blk = TQ_SW // w

    def window(bb):
        n = i * n_blk + bb
        if bb == 0:
            return pl.multiple_of(jnp.maximum(n - 1, 0) * w, w), jnp.where(n == 0, 1, 0)
        return pl.multiple_of((n - 1) * w, w), 0

    def scores(bb):
        start, tsel = window(bb)
        kwin = k_ref[pl.ds(start, 2 * w), :]
        kswp = jnp.concatenate([kwin[:, d:], kwin[:, :d]], axis=1)
        qblk = q_ref[bb * w:(bb + 1) * w, :] * jnp.asarray(d ** -0.5, BF16)
        out = []
        for c in range(SW_KV_HEADS):
            s_parts = []
            for pair in (SW_COL_ORDER[:2], SW_COL_ORDER[2:]):
                qz = []
                for g in pair:
                    hh = c * SW_GROUP + g
                    qpair = qblk[:, (hh // 2) * LANES:(hh // 2 + 1) * LANES]
                    keep = (lane >= d) if hh % 2 else (lane < d)
                    qz.append(jnp.where(keep, qpair, jnp.zeros_like(qpair)))
                kk_ = kwin if pair[0] % 2 == c else kswp
                s_parts.append(lax.dot_general(kk_, jnp.concatenate(qz, axis=0),
                                               (((1,), (1,)), ((), ())),
                                               preferred_element_type=F32))
            out.append(jnp.concatenate(s_parts, axis=1) + tab_ref[c, tsel])
        return out

    def softmax_pv(bb, u_all):
        start, _ = window(bb)
        v_t = v_ref[pl.ds(start, 2 * w), :].astype(F32).T.astype(BF16)
        o_rows = []
        for c in range(SW_KV_HEADS):
            u = u_all[c]
            sink = sink_ref[c]
            m = jnp.maximum(jnp.max(u, axis=0, keepdims=True), sink)
            p = jnp.exp(u - m)
            inv = 1.0 / (jnp.sum(p, axis=0, keepdims=True) + jnp.exp(sink - m))
            o_c = jnp.dot(v_t[c * d:(c + 1) * d, :], p.astype(BF16),
                          preferred_element_type=F32) * inv
            for g in range(SW_GROUP):
                col = SW_COL_ORDER.index(g) * w
                o_rows.append(o_c[:, col:col + w])
        o_t = jnp.concatenate(o_rows, axis=0)
        o_ref[bb * w:(bb + 1) * w, :] = o_t.T.astype(BF16)

    u_next = scores(0)
    for bb in range(n_blk):
        u_cur = u_next
        if bb + 1 < n_blk:
            u_next = scores(bb + 1)
        softmax_pv(bb, u_cur)


def _sw_attention(attn, slopes, sink_rows):
    nq = SEQ // TQ_SW
    width = SW_HEADS * SW_HEAD_DIM
    return pl.pallas_call(
        _sw_kernel,
        grid=(BATCH, nq),
        in_specs=[
            pl.BlockSpec(memory_space=pltpu.SMEM),
            pl.BlockSpec((SW_KV_HEADS, 1, SW_GROUP * SW_WINDOW), lambda b, i: (0, 0, 0)),
            pl.BlockSpec((TQ_SW, width), lambda b, i: (b * nq + i, SW_Q_BLK512)),
            pl.BlockSpec((SEQ, LANES), lambda b, i: (b, SW_K_BLK)),
            pl.BlockSpec((SEQ, LANES), lambda b, i: (b, SW_V_BLK)),
        ],
        out_specs=pl.BlockSpec((TQ_SW, width), lambda b, i: (b * nq + i, 0)),
        out_shape=jax.ShapeDtypeStruct((ROWS, width), BF16),
        scratch_shapes=[
            pltpu.VMEM((SW_KV_HEADS, 2, 2 * SW_WINDOW, SW_GROUP * SW_WINDOW), F32),
        ],
        compiler_params=pltpu.CompilerParams(
            dimension_semantics=("arbitrary", "arbitrary"), vmem_limit_bytes=VMEM_LIMIT),
        name="sw_attn",
    )(slopes, sink_rows, attn, attn, attn)


def _sigmoid(x):
    return 1.0 / (1.0 + jnp.exp(-x))


def _merge_kernel(yda_ref, ysw_ref, g_ref, x_ref, wda_ref, wsw_ref, wmix_ref, nw_ref,
                  xo_ref, xn_ref):
    t_da = jnp.dot(yda_ref[...], wda_ref[...], preferred_element_type=F32)
    t_sw = jnp.dot(ysw_ref[...], wsw_ref[...], preferred_element_type=F32)
    g_da = g_ref[:, :D_MODEL].astype(F32)
    g_sw = g_ref[:, D_MODEL:].astype(F32)
    merged = _sigmoid(g_da) * t_da + _sigmoid(g_sw) * t_sw
    xo = x_ref[...] + jnp.dot(merged.astype(BF16), wmix_ref[...], preferred_element_type=F32)
    xo_ref[...] = xo
    xn_ref[...] = _rms(xo, nw_ref[...]).astype(BF16)


def _merge(yda, ysw, gates, x, wda, wsw, wmix, nw, layer):
    half = DA_HEADS * 2 * DA_HEAD_DIM
    row = lambda i: (i, 0)
    fixed = lambda i: (0, 0)
    of_layer = lambda i: (layer, 0, 0)
    return pl.pallas_call(
        _merge_kernel,
        grid=(ROWS // TM_MERGE,),
        in_specs=[
            pl.BlockSpec((TM_MERGE, half), row),
            pl.BlockSpec((TM_MERGE, half), row),
            pl.BlockSpec((TM_MERGE, GATE_COLS), row),
            pl.BlockSpec((TM_MERGE, D_MODEL), row),
            pl.BlockSpec((None, half, D_MODEL), of_layer),
            pl.BlockSpec((None, half, D_MODEL), of_layer),
            pl.BlockSpec((None, D_MODEL, D_MODEL), of_layer),
            pl.BlockSpec((1, D_MODEL), fixed),
        ],
        out_specs=[
            pl.BlockSpec((TM_MERGE, D_MODEL), row),
            pl.BlockSpec((TM_MERGE, D_MODEL), row),
        ],
        out_shape=[
            jax.ShapeDtypeStruct((ROWS, D_MODEL), F32),
            jax.ShapeDtypeStruct((ROWS, D_MODEL), BF16),
        ],
        compiler_params=pltpu.CompilerParams(
            dimension_semantics=("arbitrary",), vmem_limit_bytes=VMEM_LIMIT),
        name="merge",
    )(yda, ysw, gates, x, wda, wsw, wmix, nw)


def _ffn_kernel(xn_ref, x_ref, wup_ref, cw_ref, cb_ref, wdn_ref, o_ref,
                carry_ref, h_ref):
    i = pl.program_id(0)
    tm = TM_FFN
    seq_start = (i % (SEQ // tm)) == 0
    sub = lax.broadcasted_iota(jnp.int32, (SUBLANES, FF_CH), 0)

    def shifted(u, prev, k):
        rolled = pltpu.roll(u, k, 0)
        head = jnp.where(sub < k, pltpu.roll(prev, k, 0), rolled[:SUBLANES])
        return jnp.concatenate([head, rolled[SUBLANES:]], axis=0)

    def conv_chunk(col0):
        sl = slice(col0, col0 + FF_CH)
        u = jnp.dot(xn_ref[...], wup_ref[:, sl], preferred_element_type=F32)
        prev = jnp.where(seq_start, 0.0, carry_ref[:, sl])
        carry_ref[:, sl] = u[tm - SUBLANES:, :]
        return (cb_ref[:, sl]
                + shifted(u, prev, 2) * cw_ref[0:1, sl]
                + shifted(u, prev, 1) * cw_ref[1:2, sl]
                + u * cw_ref[2:3, sl])

    for c in range(N_FF_CH):
        gate = conv_chunk(c * FF_CH)
        val = conv_chunk(D_FF + c * FF_CH)
        h_ref[:, c * FF_CH:(c + 1) * FF_CH] = ((gate * _sigmoid(gate)) * val).astype(BF16)

    o_ref[...] = x_ref[...] + jnp.dot(h_ref[...], wdn_ref[...], preferred_element_type=F32)


def _ffn(xn, x, wup, cw, cb, wdn, layer):
    row = lambda i: (i, 0)
    fixed = lambda i: (0, 0)
    of_layer = lambda i: (layer, 0, 0)
    return pl.pallas_call(
        _ffn_kernel,
        grid=(ROWS // TM_FFN,),
        in_specs=[
            pl.BlockSpec((TM_FFN, D_MODEL), row),
            pl.BlockSpec((TM_FFN, D_MODEL), row),
            pl.BlockSpec((None, D_MODEL, 2 * D_FF), of_layer, pipeline_mode=pl.Buffered(1)),
            pl.BlockSpec((CONV_WIDTH, 2 * D_FF), fixed),
            pl.BlockSpec((1, 2 * D_FF), fixed),
            pl.BlockSpec((None, D_FF, D_MODEL), of_layer, pipeline_mode=pl.Buffered(1)),
        ],
        out_specs=pl.BlockSpec((TM_FFN, D_MODEL), row),
        out_shape=jax.ShapeDtypeStruct((ROWS, D_MODEL), F32),
        scratch_shapes=[
            pltpu.VMEM((SUBLANES, 2 * D_FF), F32),
            pltpu.VMEM((TM_FFN, D_FF), BF16),
        ],
        compiler_params=pltpu.CompilerParams(
            dimension_semantics=("arbitrary",), vmem_limit_bytes=VMEM_LIMIT),
        name="conv_ffn",
    )(xn, x, wup, cw, cb, wdn)


def _final_norm_kernel(x_ref, nw_ref, o_ref):
    o_ref[...] = _rms(x_ref[...], nw_ref[...])


def _final_norm(x, nw):
    tm = 1024
    return pl.pallas_call(
        _final_norm_kernel,
        grid=(ROWS // tm,),
        in_specs=[pl.BlockSpec((tm, D_MODEL), lambda i: (i, 0)),
                  pl.BlockSpec((1, D_MODEL), lambda i: (0, 0))],
        out_specs=pl.BlockSpec((tm, D_MODEL), lambda i: (i, 0)),
        out_shape=jax.ShapeDtypeStruct((ROWS, D_MODEL), F32),
        compiler_params=pltpu.CompilerParams(
            dimension_semantics=("arbitrary",), vmem_limit_bytes=VMEM_LIMIT),
        name="final_norm",
    )(x, nw)


def _alibi_slopes(n_heads):
    hh = jnp.arange(1, n_heads + 1, dtype=F32)
    return jnp.exp2(-8.0 * hh / n_heads)


def kernel(x, norm_mix_w, w_in, lambda_q1, lambda_k1, lambda_q2, lambda_k2, subln_w, sinks,
           w_br_da, w_br_sw, w_mix_out, norm_ffn_w, w_up, conv_w, conv_b, w_down, norm_final_w):
    xf = x.reshape(ROWS, D_MODEL).astype(F32)
    w_in_b = w_in.astype(BF16)
    w_da_b = w_br_da.astype(BF16)
    w_sw_b = w_br_sw.astype(BF16)
    w_mix_b = w_mix_out.astype(BF16)
    w_up_b = w_up.astype(BF16)
    w_dn_b = w_down.astype(BF16)
    da_slopes = _alibi_slopes(DA_HEADS)
    sw_slopes = _alibi_slopes(SW_HEADS)

    for l in range(DEPTH):
        lam_init = 0.8 - 0.6 * math.exp(-0.3 * l)
        attn, gates = _inproj(xf, norm_mix_w[l].reshape(1, D_MODEL).astype(F32), w_in_b, l)
        da_scal = jnp.concatenate([da_slopes, jnp.full((1,), lam_init, F32)])
        vec = lambda a: a[l].reshape(1, DA_HEAD_DIM).astype(F32)
        y_da = _da_attention(attn, da_scal, vec(lambda_q1), vec(lambda_k1), vec(lambda_q2),
                             vec(lambda_k2), subln_w[l].reshape(2 * DA_HEAD_DIM, 1).astype(F32))
        sink_cols = sinks[l].astype(F32).reshape(SW_KV_HEADS, SW_GROUP)[:, list(SW_COL_ORDER)]
        sink_rows = jnp.repeat(sink_cols, SW_WINDOW, axis=1).reshape(
            SW_KV_HEADS, 1, SW_GROUP * SW_WINDOW)
        y_sw = _sw_attention(attn, sw_slopes, sink_rows)
        xf, xn = _merge(y_da, y_sw, gates, xf, w_da_b, w_sw_b, w_mix_b,
                        norm_ffn_w[l].reshape(1, D_MODEL).astype(F32), l)
        xf = _ffn(xn, xf, w_up_b, conv_w[l].astype(F32),
                  conv_b[l].reshape(1, 2 * D_FF).astype(F32), w_dn_b, l)
    out = _final_norm(xf, norm_final_w.reshape(1, D_MODEL).astype(F32))
    return out.reshape(BATCH, SEQ, D_MODEL)
```

```python
import functools
import math

import jax
import jax.numpy as jnp
import numpy as np
from jax import lax
from jax.experimental import pallas as pl
from jax.experimental.pallas import tpu as pltpu

D_MODEL = 1024
BATCH = 4
SEQ = 4096
DEPTH = 4
ROWS = BATCH * SEQ

DA_HEADS = 4
DA_HEAD_DIM = 64
SW_HEADS = 8
SW_KV_HEADS = 2
SW_HEAD_DIM = 64
SW_GROUP = SW_HEADS // SW_KV_HEADS
SW_WINDOW = 128
D_FF = 2816
CONV_WIDTH = 3
EPS = 1e-6
NEG_INF = -1e30

ATTN_COLS = 2304
GATE_COLS = 2 * D_MODEL
IN_COLS = ATTN_COLS + GATE_COLS
LANES = 128
SUBLANES = 8
DA_K_BLK = 4
DA_V_BLK = 8
SW_Q_BLK512 = 3
SW_K_BLK = 16
SW_V_BLK = 17

VMEM_LIMIT = 56 * 1024 * 1024

BF16 = jnp.bfloat16
F32 = jnp.float32

TM_PROJ = 512
PROJ_CH = 256
TQ_DA = 512
DA_TPS = SEQ // TQ_DA
TK_DA = 512
DA_VT_PAD = 16
DA_NT = 256
DA_POS_SHIFT = 8
TQ_SW = 512
SW_COL_ORDER = (0, 2, 1, 3)
TM_TAIL = 512
FF_CH = 256
N_FF_CH = D_FF // FF_CH


def _bf16_terms(x, n):
    terms, rest = [], float(x)
    for _ in range(n):
        t = float(np.asarray(rest, dtype=BF16).astype(np.float64))
        terms.append(t)
        rest -= t
    assert rest == 0.0, (x, terms)
    return terms


LOG2E = float(np.float32(1.4426950408889634))
LOG2E_TERMS = _bf16_terms(LOG2E, 3)
DA_Q_SCALE = LOG2E * DA_HEAD_DIM ** -0.5
DA_Q_COLS = DA_HEADS * 2 * DA_HEAD_DIM


def _rms(x, w):
    ms = jnp.mean(x * x, axis=-1, keepdims=True)
    return (x * lax.rsqrt(ms + EPS)) * w


def _inproj_kernel(x_ref, nw_ref, w_ref, attn_ref, gate_ref):
    xn = _rms(x_ref[...], nw_ref[...]).astype(BF16)
    for c in range(ATTN_COLS // PROJ_CH):
        sl = slice(c * PROJ_CH, (c + 1) * PROJ_CH)
        res = jnp.dot(xn, w_ref[:, sl], preferred_element_type=F32)
        if (c + 1) * PROJ_CH <= DA_Q_COLS:
            res = res * DA_Q_SCALE
        attn_ref[:, sl] = res.astype(BF16)
    for c in range(GATE_COLS // PROJ_CH):
        src = slice(ATTN_COLS + c * PROJ_CH, ATTN_COLS + (c + 1) * PROJ_CH)
        dst = slice(c * PROJ_CH, (c + 1) * PROJ_CH)
        gate_ref[:, dst] = jnp.dot(xn, w_ref[:, src], preferred_element_type=F32).astype(BF16)


def _inproj(x, nw, w, layer):
    return pl.pallas_call(
        _inproj_kernel,
        grid=(ROWS // TM_PROJ,),
        in_specs=[
            pl.BlockSpec((TM_PROJ, D_MODEL), lambda i: (i, 0)),
            pl.BlockSpec((1, D_MODEL), lambda i: (0, 0)),
            pl.BlockSpec((None, D_MODEL, IN_COLS), lambda i: (layer, 0, 0)),
        ],
        out_specs=[
            pl.BlockSpec((TM_PROJ, ATTN_COLS), lambda i: (i, 0)),
            pl.BlockSpec((TM_PROJ, GATE_COLS), lambda i: (i, 0)),
        ],
        out_shape=[
            jax.ShapeDtypeStruct((ROWS, ATTN_COLS), BF16),
            jax.ShapeDtypeStruct((ROWS, GATE_COLS), BF16),
        ],
        compiler_params=pltpu.CompilerParams(
            dimension_semantics=("arbitrary",), vmem_limit_bytes=VMEM_LIMIT),
        name="inproj",
    )(x, nw, w)


def _da_kernel(scal_ref, lq1_ref, lk1_ref, lq2_ref, lk2_ref, subw_ref, q_ref, k_ref, v_ref,
               o_ref, pos_ref, mask_ref, vt_ref, qs_ref, sa_ref, sb_ref, acc_ref, m_ref, l_ref):
    b = pl.program_id(0)
    h = pl.program_id(1)
    tq, tk, nt = TQ_DA, TK_DA, DA_NT
    tiles_per_map = tq // nt
    nt_dims = (((1,), (1,)), ((), ()))
    n_terms = len(LOG2E_TERMS)

    sub = lax.broadcasted_iota(jnp.int32, (DA_VT_PAD, tk), 0)
    ones_rows = jnp.where(sub == 0, 1.0, 0.0).astype(BF16)
    for jb in range(SEQ // tk):
        v_blk = v_ref[jb * tk:(jb + 1) * tk, :].astype(F32)
        vt_ref[jb, :2 * DA_HEAD_DIM, :] = v_blk.T.astype(BF16)
        vt_ref[jb, 2 * DA_HEAD_DIM:, :] = ones_rows

    @pl.when((b == 0) & (h == 0))
    def _():
        row = lax.broadcasted_iota(jnp.int32, (SEQ, LANES), 0)
        lane = lax.broadcasted_iota(jnp.int32, (SEQ, LANES), 1)
        hi = (row >> DA_POS_SHIFT).astype(F32)
        lo = (row & ((1 << DA_POS_SHIFT) - 1)).astype(F32)
        pos = jnp.where(lane < 2 * n_terms, jnp.where((lane & 1) == 0, hi, lo), 0.0)
        pos_ref[...] = pos.astype(BF16)
        kk = lax.broadcasted_iota(jnp.int32, (nt, nt), 0)
        qq = lax.broadcasted_iota(jnp.int32, (nt, nt), 1)
        mask_ref[...] = jnp.where(kk <= qq, 0.0, NEG_INF)

    slope = jnp.full((1, 1), scal_ref[h], F32)
    lane = lax.broadcasted_iota(jnp.int32, (tq, LANES), 1)
    slope_blk = jnp.zeros((tq, LANES), F32)
    for t, term in enumerate(LOG2E_TERMS):
        slope_blk = jnp.where(lane == 2 * t, slope * (term * (1 << DA_POS_SHIFT)), slope_blk)
        slope_blk = jnp.where(lane == 2 * t + 1, slope * term, slope_blk)
    slope_blk = slope_blk.astype(BF16)
    for k in range(DA_TPS):
        q = q_ref[k * tq:(k + 1) * tq, :]
        zero = jnp.zeros_like(q)
        qs_ref[k, :tq, :LANES] = jnp.where(lane < DA_HEAD_DIM, q, zero)
        qs_ref[k, tq:, :LANES] = jnp.where(lane >= DA_HEAD_DIM, q, zero)
        qs_ref[k, :tq, LANES:] = slope_blk
        qs_ref[k, tq:, LANES:] = slope_blk

    def reset_state():
        m_ref[...] = jnp.full(m_ref.shape, NEG_INF, F32)
        l_ref[...] = jnp.zeros(l_ref.shape, F32)
        acc_ref[...] = jnp.zeros(acc_ref.shape, F32)

    n_cols = 2 * tiles_per_map
    halves = (range(n_cols // 2), range(n_cols // 2, n_cols))

    def scores(k, j, s_ref, tiles=range(n_cols)):
        keys = pl.ds(pl.multiple_of(j * tk, tk), tk) if isinstance(j, jax.Array) \
            else slice(j * tk, (j + 1) * tk)
        kx = jnp.concatenate([k_ref[keys, :], pos_ref[keys, :]], axis=1)
        for c in tiles:
            cols = slice(c * nt, (c + 1) * nt)
            s_ref[:, cols] = lax.dot_general(kx, qs_ref[k, cols, :], nt_dims,
                                             preferred_element_type=F32)

    def softmax_pv(j, s_ref, diagonal, tiles=range(n_cols)):
        for c in tiles:
            cols = slice(c * nt, (c + 1) * nt)
            n_sub = c % tiles_per_map + 1 if diagonal else tk // nt
            rows = n_sub * nt
            s = s_ref[:rows, cols]
            if diagonal and rows == nt:
                s = s + mask_ref[...]
            elif diagonal:
                s = jnp.concatenate([s[:rows - nt], s[rows - nt:] + mask_ref[...]], axis=0)
            m_old = m_ref[:, cols]
            m_new = jnp.maximum(m_old, jnp.max(s, axis=0, keepdims=True))
            alpha = jnp.exp2(m_old - m_new)
            p = jnp.exp2(s - m_new).astype(BF16)
            pv = jnp.dot(vt_ref[j, :, :rows], p, preferred_element_type=F32)
            acc_ref[:, cols] = alpha * acc_ref[:, cols] + pv[:2 * DA_HEAD_DIM]
            l_ref[:, cols] = alpha * l_ref[:, cols] + pv[2 * DA_HEAD_DIM:2 * DA_HEAD_DIM + 1]
            m_ref[:, cols] = m_new

    def overlapped(k_next, j_next, next_ref, j_cur, cur_ref, diagonal=False):
        for half in halves:
            scores(k_next, j_next, next_ref, half)
            softmax_pv(j_cur, cur_ref, diagonal, half)

    lam_init = jnp.full((1, 1), scal_ref[DA_HEADS], F32)
    lam = (jnp.exp(jnp.sum(lq1_ref[...] * lk1_ref[...], axis=-1, keepdims=True))
           - jnp.exp(jnp.sum(lq2_ref[...] * lk2_ref[...], axis=-1, keepdims=True)) + lam_init)

    def finish(k):
        inv = 1.0 / l_ref[...]
        acc = acc_ref[...]
        o = acc[:, :tq] * inv[:, :tq] - lam * (acc[:, tq:] * inv[:, tq:])
        ms = jnp.mean(o * o, axis=0, keepdims=True)
        y = ((o * lax.rsqrt(ms + EPS)) * subw_ref[...]) * (1.0 - lam_init)
        o_ref[k * tq:(k + 1) * tq, :] = y.T.astype(BF16)

    bufs = (sa_ref, sb_ref)
    cur = 0
    reset_state()
    scores(0, 0, bufs[cur])
    for k in range(DA_TPS):
        buf_a, buf_b = bufs[cur], bufs[1 - cur]

        def pair(t, carry, k=k, buf_a=buf_a, buf_b=buf_b):
            overlapped(k, 2 * t + 1, buf_b, 2 * t, buf_a)
            overlapped(k, 2 * t + 2, buf_a, 2 * t + 1, buf_b)
            return carry

        lax.fori_loop(0, k // 2, pair, 0)
        if k % 2 == 1:
            overlapped(k, k, buf_b, k - 1, buf_a)
            diag_buf, free_buf = buf_b, buf_a
        else:
            diag_buf, free_buf = buf_a, buf_b
        if k + 1 < DA_TPS:
            overlapped(k + 1, 0, free_buf, k, diag_buf, diagonal=True)
            finish(k)
            reset_state()
            cur = 0 if free_buf is bufs[0] else 1
        else:
            softmax_pv(k, diag_buf, True)
            finish(k)


def _da_attention(attn, scal, lq1, lk1, lq2, lk2, subw):
    vec = pl.BlockSpec((1, DA_HEAD_DIM), lambda b, h: (0, 0))
    return pl.pallas_call(
        _da_kernel,
        grid=(BATCH, DA_HEADS),
        in_specs=[
            pl.BlockSpec(memory_space=pltpu.SMEM),
            vec, vec, vec, vec,
            pl.BlockSpec((2 * DA_HEAD_DIM, 1), lambda b, h: (0, 0)),
            pl.BlockSpec((SEQ, LANES), lambda b, h: (b, h)),
            pl.BlockSpec((SEQ, LANES), lambda b, h: (b, DA_K_BLK + h)),
            pl.BlockSpec((SEQ, LANES), lambda b, h: (b, DA_V_BLK + h)),
        ],
        out_specs=pl.BlockSpec((SEQ, LANES), lambda b, h: (b, h)),
        out_shape=jax.ShapeDtypeStruct((ROWS, DA_HEADS * 2 * DA_HEAD_DIM), BF16),
        scratch_shapes=[
            pltpu.VMEM((SEQ, LANES), BF16),
            pltpu.VMEM((DA_NT, DA_NT), F32),
            pltpu.VMEM((SEQ // TK_DA, 2 * DA_HEAD_DIM + DA_VT_PAD, TK_DA), BF16),
            pltpu.VMEM((DA_TPS, 2 * TQ_DA, 2 * LANES), BF16),
            pltpu.VMEM((TK_DA, 2 * TQ_DA), F32),
            pltpu.VMEM((TK_DA, 2 * TQ_DA), F32),
            pltpu.VMEM((2 * DA_HEAD_DIM, 2 * TQ_DA), F32),
            pltpu.VMEM((1, 2 * TQ_DA), F32),
            pltpu.VMEM((1, 2 * TQ_DA), F32),
        ],
        compiler_params=pltpu.CompilerParams(
            dimension_semantics=("arbitrary", "arbitrary"),
            vmem_limit_bytes=VMEM_LIMIT),
        name="diff_attn",
    )(scal, lq1, lk1, lq2, lk2, subw, attn, attn, attn)


def _sw_kernel(slope_ref, sink_ref, q_ref, k_ref, v_ref, o_ref, tab_ref):
    b = pl.program_id(0)
    i = pl.program_id(1)
    w = SW_WINDOW
    d = SW_HEAD_DIM
    ncol = SW_GROUP * w

    @pl.when((b == 0) & (i == 0))
    def _():
        kk = lax.broadcasted_iota(jnp.int32, (2 * w, ncol), 0)
        col = lax.broadcasted_iota(jnp.int32, (2 * w, ncol), 1)
        qq = col & (w - 1)
        grp = col >> 7
        for c in range(SW_KV_HEADS):
            slope = jnp.zeros((2 * w, ncol), F32)
            for pos, g in enumerate(SW_COL_ORDER):
                slope = jnp.where(grp == pos, slope_ref[c * SW_GROUP + g], slope)
            dist = qq + w - kk
            tab_ref[c, 0] = jnp.where((dist >= 0) & (dist < w), -slope * dist.astype(F32), NEG_INF)
            dist0 = qq - kk
            tab_ref[c, 1] = jnp.where(dist0 >= 0, -slope * dist0.astype(F32), NEG_INF)

    lane = lax.broadcasted_iota(jnp.int32, (w, LANES), 1)
    n_blk = TQ_SW // w

    def window(bb):
        n = i * n_blk + bb
        if bb == 0:
            return pl.multiple_of(jnp.maximum(n - 1, 0) * w, w), jnp.where(n == 0, 1, 0)
        return pl.multiple_of((n - 1) * w, w), 0

    def scores(bb):
        start, tsel = window(bb)
        kwin = k_ref[pl.ds(start, 2 * w), :]
        kswp = jnp.concatenate([kwin[:, d:], kwin[:, :d]], axis=1)
        qblk = q_ref[bb * w:(bb + 1) * w, :] * jnp.asarray(d ** -0.5, BF16)
        out = []
        for c in range(SW_KV_HEADS):
            s_parts = []
            for pair in (SW_COL_ORDER[:2], SW_COL_ORDER[2:]):
                qz = []
                for g in pair:
                    hh = c * SW_GROUP + g
                    qpair = qblk[:, (hh // 2) * LANES:(hh // 2 + 1) * LANES]
                    keep = (lane >= d) if hh % 2 else (lane < d)
                    qz.append(jnp.where(keep, qpair, jnp.zeros_like(qpair)))
                kk_ = kwin if pair[0] % 2 == c else kswp
                s_parts.append(lax.dot_general(kk_, jnp.concatenate(qz, axis=0),
                                               (((1,), (1,)), ((), ())),
                                               preferred_element_type=F32))
            out.append(jnp.concatenate(s_parts, axis=1) + tab_ref[c, tsel])
        return out

    def softmax_pv(bb, u_all):
        start, _ = window(bb)
        v_t = v_ref[pl.ds(start, 2 * w), :].astype(F32).T.astype(BF16)
        o_rows = []
        for c in range(SW_KV_HEADS):
            u = u_all[c]
            sink = sink_ref[c]
            m = jnp.maximum(jnp.max(u, axis=0, keepdims=True), sink)
            p = jnp.exp(u - m)
            inv = 1.0 / (jnp.sum(p, axis=0, keepdims=True) + jnp.exp(sink - m))
            o_c = jnp.dot(v_t[c * d:(c + 1) * d, :], p.astype(BF16),
                          preferred_element_type=F32) * inv
            for g in range(SW_GROUP):
                col = SW_COL_ORDER.index(g) * w
                o_rows.append(o_c[:, col:col + w])
        o_t = jnp.concatenate(o_rows, axis=0)
        o_ref[bb * w:(bb + 1) * w, :] = o_t.T.astype(BF16)

    u_next = scores(0)
    for bb in range(n_blk):
        u_cur = u_next
        if bb + 1 < n_blk:
            u_next = scores(bb + 1)
        softmax_pv(bb, u_cur)


def _sw_attention(attn, slopes, sink_rows):
    nq = SEQ // TQ_SW
    width = SW_HEADS * SW_HEAD_DIM
    return pl.pallas_call(
        _sw_kernel,
        grid=(BATCH, nq),
        in_specs=[
            pl.BlockSpec(memory_space=pltpu.SMEM),
            pl.BlockSpec((SW_KV_HEADS, 1, SW_GROUP * SW_WINDOW), lambda b, i: (0, 0, 0)),
            pl.BlockSpec((TQ_SW, width), lambda b, i: (b * nq + i, SW_Q_BLK512)),
            pl.BlockSpec((SEQ, LANES), lambda b, i: (b, SW_K_BLK)),
            pl.BlockSpec((SEQ, LANES), lambda b, i: (b, SW_V_BLK)),
        ],
        out_specs=pl.BlockSpec((TQ_SW, width), lambda b, i: (b * nq + i, 0)),
        out_shape=jax.ShapeDtypeStruct((ROWS, width), BF16),
        scratch_shapes=[
            pltpu.VMEM((SW_KV_HEADS, 2, 2 * SW_WINDOW, SW_GROUP * SW_WINDOW), F32),
        ],
        compiler_params=pltpu.CompilerParams(
            dimension_semantics=("arbitrary", "arbitrary"), vmem_limit_bytes=VMEM_LIMIT),
        name="sw_attn",
    )(slopes, sink_rows, attn, attn, attn)


def _sigmoid(x):
    return 1.0 / (1.0 + jnp.exp(-x))


def _tail_kernel(yda_ref, ysw_ref, g_ref, x_ref, wda_ref, wsw_ref, wmix_ref, nw_ref,
                 wup_ref, cw_ref, cb_ref, wdn_ref, fw_ref, o_ref, carry_ref, *, final):
    i = pl.program_id(0)
    tm = TM_TAIL
    seq_start = (i % (SEQ // tm)) == 0
    sub = lax.broadcasted_iota(jnp.int32, (SUBLANES, FF_CH), 0)

    t_da = jnp.dot(yda_ref[...], wda_ref[...], preferred_element_type=F32)
    t_sw = jnp.dot(ysw_ref[...], wsw_ref[...], preferred_element_type=F32)
    g_da = g_ref[:, :D_MODEL].astype(F32)
    g_sw = g_ref[:, D_MODEL:].astype(F32)
    merged = _sigmoid(g_da) * t_da + _sigmoid(g_sw) * t_sw
    x_mid = x_ref[...] + jnp.dot(merged.astype(BF16), wmix_ref[...], preferred_element_type=F32)
    o_ref[...] = x_mid
    xn = _rms(x_mid, nw_ref[...]).astype(BF16)

    def shifted(u, prev, k):
        rolled = pltpu.roll(u, k, 0)
        head = jnp.where(sub < k, pltpu.roll(prev, k, 0), rolled[:SUBLANES])
        return jnp.concatenate([head, rolled[SUBLANES:]], axis=0)

    def conv_chunk(col0):
        sl = slice(col0, col0 + FF_CH)
        u = jnp.dot(xn, wup_ref[:, sl], preferred_element_type=F32)
        prev = jnp.where(seq_start, 0.0, carry_ref[:, sl])
        carry_ref[:, sl] = u[tm - SUBLANES:, :]
        return (cb_ref[:, sl]
                + shifted(u, prev, 2) * cw_ref[0:1, sl]
                + shifted(u, prev, 1) * cw_ref[1:2, sl]
                + u * cw_ref[2:3, sl])

    h = []
    for c in range(N_FF_CH):
        gate = conv_chunk(c * FF_CH)
        val = conv_chunk(D_FF + c * FF_CH)
        h.append(((gate * _sigmoid(gate)) * val).astype(BF16))
    h = jnp.concatenate(h, axis=1)

    out = o_ref[...] + jnp.dot(h, wdn_ref[...], preferred_element_type=F32)
    if final:
        out = _rms(out, fw_ref[...])
    o_ref[...] = out


def _layer_tail(yda, ysw, gates, x, wda, wsw, wmix, nw, wup, cw, cb, wdn, fw, layer):
    half = DA_HEADS * 2 * DA_HEAD_DIM
    row = lambda i: (i, 0)
    fixed = lambda i: (0, 0)
    of_layer = lambda i: (layer, 0, 0)
    resident = dict(pipeline_mode=pl.Buffered(1))
    return pl.pallas_call(
        functools.partial(_tail_kernel, final=layer == DEPTH - 1),
        grid=(ROWS // TM_TAIL,),
        in_specs=[
            pl.BlockSpec((TM_TAIL, half), row),
            pl.BlockSpec((TM_TAIL, half), row),
            pl.BlockSpec((TM_TAIL, GATE_COLS), row),
            pl.BlockSpec((TM_TAIL, D_MODEL), row),
            pl.BlockSpec((None, half, D_MODEL), of_layer, **resident),
            pl.BlockSpec((None, half, D_MODEL), of_layer, **resident),
            pl.BlockSpec((None, D_MODEL, D_MODEL), of_layer, **resident),
            pl.BlockSpec((1, D_MODEL), fixed),
            pl.BlockSpec((None, D_MODEL, 2 * D_FF), of_layer, **resident),
            pl.BlockSpec((CONV_WIDTH, 2 * D_FF), fixed),
            pl.BlockSpec((1, 2 * D_FF), fixed),
            pl.BlockSpec((None, D_FF, D_MODEL), of_layer, **resident),
            pl.BlockSpec((1, D_MODEL), fixed),
        ],
        out_specs=pl.BlockSpec((TM_TAIL, D_MODEL), row),
        out_shape=jax.ShapeDtypeStruct((ROWS, D_MODEL), F32),
        scratch_shapes=[
            pltpu.VMEM((SUBLANES, 2 * D_FF), F32),
        ],
        compiler_params=pltpu.CompilerParams(
            dimension_semantics=("arbitrary",), vmem_limit_bytes=VMEM_LIMIT),
        name="layer_tail",
    )(yda, ysw, gates, x, wda, wsw, wmix, nw, wup, cw, cb, wdn, fw)


def _alibi_slopes(n_heads):
    hh = jnp.arange(1, n_heads + 1, dtype=F32)
    return jnp.exp2(-8.0 * hh / n_heads)


def kernel(x, norm_mix_w, w_in, lambda_q1, lambda_k1, lambda_q2, lambda_k2, subln_w, sinks,
           w_br_da, w_br_sw, w_mix_out, norm_ffn_w, w_up, conv_w, conv_b, w_down, norm_final_w):
    xf = x.reshape(ROWS, D_MODEL).astype(F32)
    w_in_b = w_in.astype(BF16)
    w_da_b = w_br_da.astype(BF16)
    w_sw_b = w_br_sw.astype(BF16)
    w_mix_b = w_mix_out.astype(BF16)
    w_up_b = w_up.astype(BF16)
    w_dn_b = w_down.astype(BF16)
    da_slopes = _alibi_slopes(DA_HEADS)
    sw_slopes = _alibi_slopes(SW_HEADS)

    for l in range(DEPTH):
        lam_init = 0.8 - 0.6 * math.exp(-0.3 * l)
        attn, gates = _inproj(xf, norm_mix_w[l].reshape(1, D_MODEL).astype(F32), w_in_b, l)
        da_scal = jnp.concatenate([da_slopes, jnp.full((1,), lam_init, F32)])
        vec = lambda a: a[l].reshape(1, DA_HEAD_DIM).astype(F32)
        y_da = _da_attention(attn, da_scal, vec(lambda_q1), vec(lambda_k1), vec(lambda_q2),
                             vec(lambda_k2), subln_w[l].reshape(2 * DA_HEAD_DIM, 1).astype(F32))
        sink_cols = sinks[l].astype(F32).reshape(SW_KV_HEADS, SW_GROUP)[:, list(SW_COL_ORDER)]
        sink_rows = jnp.repeat(sink_cols, SW_WINDOW, axis=1).reshape(
            SW_KV_HEADS, 1, SW_GROUP * SW_WINDOW)
        y_sw = _sw_attention(attn, sw_slopes, sink_rows)
        xf = _layer_tail(y_da, y_sw, gates, xf, w_da_b, w_sw_b, w_mix_b,
                         norm_ffn_w[l].reshape(1, D_MODEL).astype(F32), w_up_b,
                         conv_w[l].astype(F32), conv_b[l].reshape(1, 2 * D_FF).astype(F32),
                         w_dn_b, norm_final_w.reshape(1, D_MODEL).astype(F32), l)
    return xf.reshape(BATCH, SEQ, D_MODEL)
```

```python
import functools
import math

import jax
import jax.numpy as jnp
import numpy as np
from jax import lax
from jax.experimental import pallas as pl
from jax.experimental.pallas import tpu as pltpu

D_MODEL = 1024
BATCH = 4
SEQ = 4096
DEPTH = 4
ROWS = BATCH * SEQ

DA_HEADS = 4
DA_HEAD_DIM = 64
SW_HEADS = 8
SW_KV_HEADS = 2
SW_HEAD_DIM = 64
SW_GROUP = SW_HEADS // SW_KV_HEADS
SW_WINDOW = 128
D_FF = 2816
CONV_WIDTH = 3
EPS = 1e-6
NEG_INF = -1e30

ATTN_COLS = 2304
GATE_COLS = 2 * D_MODEL
IN_COLS = ATTN_COLS + GATE_COLS
LANES = 128
SUBLANES = 8
DA_K_BLK = 4
DA_V_BLK = 8
SW_Q_BLK512 = 3
SW_K_BLK = 16
SW_V_BLK = 17

VMEM_LIMIT = 56 * 1024 * 1024

BF16 = jnp.bfloat16
F32 = jnp.float32

TM_PROJ = 512
PROJ_CH = 256
TQ_DA = 1024
DA_TPS = SEQ // TQ_DA
TK_DA = 512
DA_VT_PAD = 16
DA_NT = 256
DA_POS_SHIFT = 8
TQ_SW = 512
SW_COL_ORDER = (0, 2, 1, 3)
TM_TAIL = 512
FF_CH = 256
N_FF_CH = D_FF // FF_CH


def _bf16_terms(x, n):
    terms, rest = [], float(x)
    for _ in range(n):
        t = float(np.asarray(rest, dtype=BF16).astype(np.float64))
        terms.append(t)
        rest -= t
    assert rest == 0.0, (x, terms)
    return terms


LOG2E = float(np.float32(1.4426950408889634))
LOG2E_TERMS = _bf16_terms(LOG2E, 3)
DA_Q_SCALE = LOG2E * DA_HEAD_DIM ** -0.5
DA_Q_COLS = DA_HEADS * 2 * DA_HEAD_DIM


def _rms(x, w):
    ms = jnp.mean(x * x, axis=-1, keepdims=True)
    return (x * lax.rsqrt(ms + EPS)) * w


def _inproj_kernel(x_ref, nw_ref, w_ref, attn_ref, gate_ref):
    xn = _rms(x_ref[...], nw_ref[...]).astype(BF16)
    for c in range(ATTN_COLS // PROJ_CH):
        sl = slice(c * PROJ_CH, (c + 1) * PROJ_CH)
        res = jnp.dot(xn, w_ref[:, sl], preferred_element_type=F32)
        if (c + 1) * PROJ_CH <= DA_Q_COLS:
            res = res * DA_Q_SCALE
        attn_ref[:, sl] = res.astype(BF16)
    for c in range(GATE_COLS // PROJ_CH):
        src = slice(ATTN_COLS + c * PROJ_CH, ATTN_COLS + (c + 1) * PROJ_CH)
        dst = slice(c * PROJ_CH, (c + 1) * PROJ_CH)
        gate_ref[:, dst] = jnp.dot(xn, w_ref[:, src], preferred_element_type=F32).astype(BF16)


def _inproj(x, nw, w, layer):
    return pl.pallas_call(
        _inproj_kernel,
        grid=(ROWS // TM_PROJ,),
        in_specs=[
            pl.BlockSpec((TM_PROJ, D_MODEL), lambda i: (i, 0)),
            pl.BlockSpec((1, D_MODEL), lambda i: (0, 0)),
            pl.BlockSpec((None, D_MODEL, IN_COLS), lambda i: (layer, 0, 0)),
        ],
        out_specs=[
            pl.BlockSpec((TM_PROJ, ATTN_COLS), lambda i: (i, 0)),
            pl.BlockSpec((TM_PROJ, GATE_COLS), lambda i: (i, 0)),
        ],
        out_shape=[
            jax.ShapeDtypeStruct((ROWS, ATTN_COLS), BF16),
            jax.ShapeDtypeStruct((ROWS, GATE_COLS), BF16),
        ],
        compiler_params=pltpu.CompilerParams(
            dimension_semantics=("arbitrary",), vmem_limit_bytes=VMEM_LIMIT),
        name="inproj",
    )(x, nw, w)


def _da_kernel(scal_ref, lq1_ref, lk1_ref, lq2_ref, lk2_ref, subw_ref, q_ref, k_ref, v_ref,
               o_ref, pos_ref, mask_ref, vt_ref, qs_ref, sa_ref, sb_ref, sa_max_ref, sb_max_ref,
               acc_ref, m_ref, l_ref):
    b = pl.program_id(0)
    h = pl.program_id(1)
    tq, tk, nt = TQ_DA, TK_DA, DA_NT
    tiles_per_map = tq // nt
    nt_dims = (((1,), (1,)), ((), ()))
    n_terms = len(LOG2E_TERMS)

    sub = lax.broadcasted_iota(jnp.int32, (DA_VT_PAD, tk), 0)
    ones_rows = jnp.where(sub == 0, 1.0, 0.0).astype(BF16)
    for jb in range(SEQ // tk):
        v_blk = v_ref[jb * tk:(jb + 1) * tk, :].astype(F32)
        vt_ref[jb, :2 * DA_HEAD_DIM, :] = v_blk.T.astype(BF16)
        vt_ref[jb, 2 * DA_HEAD_DIM:, :] = ones_rows

    @pl.when((b == 0) & (h == 0))
    def _():
        row = lax.broadcasted_iota(jnp.int32, (SEQ, LANES), 0)
        lane = lax.broadcasted_iota(jnp.int32, (SEQ, LANES), 1)
        hi = (row >> DA_POS_SHIFT).astype(F32)
        lo = (row & ((1 << DA_POS_SHIFT) - 1)).astype(F32)
        pos = jnp.where(lane < 2 * n_terms, jnp.where((lane & 1) == 0, hi, lo), 0.0)
        pos_ref[...] = pos.astype(BF16)
        kk = lax.broadcasted_iota(jnp.int32, (nt, nt), 0)
        qq = lax.broadcasted_iota(jnp.int32, (nt, nt), 1)
        mask_ref[...] = jnp.where(kk <= qq, 0.0, NEG_INF)

    slope = jnp.full((1, 1), scal_ref[h], F32)
    lane = lax.broadcasted_iota(jnp.int32, (tq, LANES), 1)
    slope_blk = jnp.zeros((tq, LANES), F32)
    for t, term in enumerate(LOG2E_TERMS):
        slope_blk = jnp.where(lane == 2 * t, slope * (term * (1 << DA_POS_SHIFT)), slope_blk)
        slope_blk = jnp.where(lane == 2 * t + 1, slope * term, slope_blk)
    slope_blk = slope_blk.astype(BF16)
    for k in range(DA_TPS):
        q = q_ref[k * tq:(k + 1) * tq, :]
        zero = jnp.zeros_like(q)
        qs_ref[k, :tq, :LANES] = jnp.where(lane < DA_HEAD_DIM, q, zero)
        qs_ref[k, tq:, :LANES] = jnp.where(lane >= DA_HEAD_DIM, q, zero)
        qs_ref[k, :tq, LANES:] = slope_blk
        qs_ref[k, tq:, LANES:] = slope_blk

    def reset_state():
        m_ref[...] = jnp.full(m_ref.shape, NEG_INF, F32)
        l_ref[...] = jnp.zeros(l_ref.shape, F32)
        acc_ref[...] = jnp.zeros(acc_ref.shape, F32)

    n_cols = 2 * tiles_per_map
    halves = (range(n_cols // 2), range(n_cols // 2, n_cols))

    subs = tk // nt

    def visible(c, first_sub):
        if first_sub is None:
            return subs, False
        diag = c % tiles_per_map - first_sub
        return max(0, min(subs, diag + 1)), 0 <= diag < subs

    def scores(k, j, buf, tiles=range(n_cols), first_sub=None):
        s_ref, smax_ref = buf
        keys = pl.ds(pl.multiple_of(j * tk, tk), tk) if isinstance(j, jax.Array) \
            else slice(j * tk, (j + 1) * tk)
        kx = jnp.concatenate([k_ref[keys, :], pos_ref[keys, :]], axis=1)
        for c in tiles:
            if visible(c, first_sub)[0] == 0:
                continue
            cols = slice(c * nt, (c + 1) * nt)
            s = lax.dot_general(kx, qs_ref[k, cols, :], nt_dims,
                                preferred_element_type=F32)
            s_ref[:, cols] = s
            smax_ref[:, cols] = jnp.max(s, axis=0, keepdims=True)

    def softmax_pv(j, buf, first_sub, tiles=range(n_cols)):
        s_ref, smax_ref = buf
        for c in tiles:
            n_sub, causal = visible(c, first_sub)
            if n_sub == 0:
                continue
            cols = slice(c * nt, (c + 1) * nt)
            rows = n_sub * nt
            s = s_ref[:rows, cols]
            if causal and rows == nt:
                s = s + mask_ref[...]
            elif causal:
                s = jnp.concatenate([s[:rows - nt], s[rows - nt:] + mask_ref[...]], axis=0)
            s_max = smax_ref[:, cols] if first_sub is None else jnp.max(s, axis=0, keepdims=True)
            m_old = m_ref[:, cols]
            m_new = jnp.maximum(m_old, s_max)
            alpha = jnp.exp2(m_old - m_new)
            p = jnp.exp2(s - m_new).astype(BF16)
            pv = jnp.dot(vt_ref[j, :, :rows], p, preferred_element_type=F32)
            acc_ref[:, cols] = alpha * acc_ref[:, cols] + pv[:2 * DA_HEAD_DIM]
            l_ref[:, cols] = alpha * l_ref[:, cols] + pv[2 * DA_HEAD_DIM:2 * DA_HEAD_DIM + 1]
            m_ref[:, cols] = m_new

    def overlapped(k_next, j_next, next_ref, j_cur, cur_ref, next_sub=None, cur_sub=None):
        for half in halves:
            scores(k_next, j_next, next_ref, half, next_sub)
            softmax_pv(j_cur, cur_ref, cur_sub, half)

    lam_init = jnp.full((1, 1), scal_ref[DA_HEADS], F32)
    lam = (jnp.exp(jnp.sum(lq1_ref[...] * lk1_ref[...], axis=-1, keepdims=True))
           - jnp.exp(jnp.sum(lq2_ref[...] * lk2_ref[...], axis=-1, keepdims=True)) + lam_init)

    def finish(k):
        inv = 1.0 / l_ref[...]
        acc = acc_ref[...]
        o = acc[:, :tq] * inv[:, :tq] - lam * (acc[:, tq:] * inv[:, tq:])
        ms = jnp.mean(o * o, axis=0, keepdims=True)
        y = ((o * lax.rsqrt(ms + EPS)) * subw_ref[...]) * (1.0 - lam_init)
        o_ref[k * tq:(k + 1) * tq, :] = y.T.astype(BF16)

    assert tq == 2 * tk
    buf_a, buf_b = (sa_ref, sa_max_ref), (sb_ref, sb_max_ref)
    reset_state()
    scores(0, 0, buf_a, first_sub=0)
    for k in range(DA_TPS):

        def pair(t, carry, k=k):
            overlapped(k, 2 * t + 1, buf_b, 2 * t, buf_a)
            overlapped(k, 2 * t + 2, buf_a, 2 * t + 1, buf_b)
            return carry

        lax.fori_loop(0, k, pair, 0)
        overlapped(k, 2 * k + 1, buf_b, 2 * k, buf_a, next_sub=subs, cur_sub=0)
        if k + 1 < DA_TPS:
            overlapped(k + 1, 0, buf_a, 2 * k + 1, buf_b, cur_sub=subs)
            finish(k)
            reset_state()
        else:
            softmax_pv(2 * k + 1, buf_b, subs)
            finish(k)


def _da_attention(attn, scal, lq1, lk1, lq2, lk2, subw):
    vec = pl.BlockSpec((1, DA_HEAD_DIM), lambda b, h: (0, 0))
    return pl.pallas_call(
        _da_kernel,
        grid=(BATCH, DA_HEADS),
        in_specs=[
            pl.BlockSpec(memory_space=pltpu.SMEM),
            vec, vec, vec, vec,
            pl.BlockSpec((2 * DA_HEAD_DIM, 1), lambda b, h: (0, 0)),
            pl.BlockSpec((SEQ, LANES), lambda b, h: (b, h)),
            pl.BlockSpec((SEQ, LANES), lambda b, h: (b, DA_K_BLK + h)),
            pl.BlockSpec((SEQ, LANES), lambda b, h: (b, DA_V_BLK + h)),
        ],
        out_specs=pl.BlockSpec((SEQ, LANES), lambda b, h: (b, h)),
        out_shape=jax.ShapeDtypeStruct((ROWS, DA_HEADS * 2 * DA_HEAD_DIM), BF16),
        scratch_shapes=[
            pltpu.VMEM((SEQ, LANES), BF16),
            pltpu.VMEM((DA_NT, DA_NT), F32),
            pltpu.VMEM((SEQ // TK_DA, 2 * DA_HEAD_DIM + DA_VT_PAD, TK_DA), BF16),
            pltpu.VMEM((DA_TPS, 2 * TQ_DA, 2 * LANES), BF16),
            pltpu.VMEM((TK_DA, 2 * TQ_DA), F32),
            pltpu.VMEM((TK_DA, 2 * TQ_DA), F32),
            pltpu.VMEM((1, 2 * TQ_DA), F32),
            pltpu.VMEM((1, 2 * TQ_DA), F32),
            pltpu.VMEM((2 * DA_HEAD_DIM, 2 * TQ_DA), F32),
            pltpu.VMEM((1, 2 * TQ_DA), F32),
            pltpu.VMEM((1, 2 * TQ_DA), F32),
        ],
        compiler_params=pltpu.CompilerParams(
            dimension_semantics=("arbitrary", "arbitrary"),
            vmem_limit_bytes=VMEM_LIMIT),
        name="diff_attn",
    )(scal, lq1, lk1, lq2, lk2, subw, attn, attn, attn)


def _sw_kernel(slope_ref, sink_ref, q_ref, k_ref, v_ref, o_ref, tab_ref):
    b = pl.program_id(0)
    i = pl.program_id(1)
    w = SW_WINDOW
    d = SW_HEAD_DIM
    ncol = SW_GROUP * w

    @pl.when((b == 0) & (i == 0))
    def _():
        kk = lax.broadcasted_iota(jnp.int32, (2 * w, ncol), 0)
        col = lax.broadcasted_iota(jnp.int32, (2 * w, ncol), 1)
        qq = col & (w - 1)
        grp = col >> 7
        for c in range(SW_KV_HEADS):
            slope = jnp.zeros((2 * w, ncol), F32)
            for pos, g in enumerate(SW_COL_ORDER):
                slope = jnp.where(grp == pos, slope_ref[c * SW_GROUP + g], slope)
            dist = qq + w - kk
            tab_ref[c, 0] = jnp.where((dist >= 0) & (dist < w), -slope * dist.astype(F32), NEG_INF)
            dist0 = qq - kk
            tab_ref[c, 1] = jnp.where(dist0 >= 0, -slope * dist0.astype(F32), NEG_INF)

    lane = lax.broadcasted_iota(jnp.int32, (w, LANES), 1)
    n_blk = TQ_SW // w

    def window(bb):
        n = i * n_blk + bb
        if bb == 0:
            return pl.multiple_of(jnp.maximum(n - 1, 0) * w, w), jnp.where(n == 0, 1, 0)
        return pl.multiple_of((n - 1) * w, w), 0

    def scores(bb):
        start, tsel = window(bb)
        kwin = k_ref[pl.ds(start, 2 * w), :]
        kswp = jnp.concatenate([kwin[:, d:], kwin[:, :d]], axis=1)
        qblk = q_ref[bb * w:(bb + 1) * w, :] * jnp.asarray(d ** -0.5, BF16)
        out = []
        for c in range(SW_KV_HEADS):
            s_parts = []
            for pair in (SW_COL_ORDER[:2], SW_COL_ORDER[2:]):
                qz = []
                for g in pair:
                    hh = c * SW_GROUP + g
                    qpair = qblk[:, (hh // 2) * LANES:(hh // 2 + 1) * LANES]
                    keep = (lane >= d) if hh % 2 else (lane < d)
                    qz.append(jnp.where(keep, qpair, jnp.zeros_like(qpair)))
                kk_ = kwin if pair[0] % 2 == c else kswp
                s_parts.append(lax.dot_general(kk_, jnp.concatenate(qz, axis=0),
                                               (((1,), (1,)), ((), ())),
                                               preferred_element_type=F32))
            out.append(jnp.concatenate(s_parts, axis=1) + tab_ref[c, tsel])
        return out

    def softmax_pv(bb, u_all):
        start, _ = window(bb)
        v_t = v_ref[pl.ds(start, 2 * w), :].astype(F32).T.astype(BF16)
        o_rows = []
        for c in range(SW_KV_HEADS):
            u = u_all[c]
            sink = sink_ref[c]
            m = jnp.maximum(jnp.max(u, axis=0, keepdims=True), sink)
            p = jnp.exp(u - m)
            inv = 1.0 / (jnp.sum(p, axis=0, keepdims=True) + jnp.exp(sink - m))
            o_c = jnp.dot(v_t[c * d:(c + 1) * d, :], p.astype(BF16),
                          preferred_element_type=F32) * inv
            for g in range(SW_GROUP):
                col = SW_COL_ORDER.index(g) * w
                o_rows.append(o_c[:, col:col + w])
        o_t = jnp.concatenate(o_rows, axis=0)
        o_ref[bb * w:(bb + 1) * w, :] = o_t.T.astype(BF16)

    u_next = scores(0)
    for bb in range(n_blk):
        u_cur = u_next
        if bb + 1 < n_blk:
            u_next = scores(bb + 1)
        softmax_pv(bb, u_cur)


def _sw_attention(attn, slopes, sink_rows):
    nq = SEQ // TQ_SW
    width = SW_HEADS * SW_HEAD_DIM
    return pl.pallas_call(
        _sw_kernel,
        grid=(BATCH, nq),
        in_specs=[
            pl.BlockSpec(memory_space=pltpu.SMEM),
            pl.BlockSpec((SW_KV_HEADS, 1, SW_GROUP * SW_WINDOW), lambda b, i: (0, 0, 0)),
            pl.BlockSpec((TQ_SW, width), lambda b, i: (b * nq + i, SW_Q_BLK512)),
            pl.BlockSpec((SEQ, LANES), lambda b, i: (b, SW_K_BLK)),
            pl.BlockSpec((SEQ, LANES), lambda b, i: (b, SW_V_BLK)),
        ],
        out_specs=pl.BlockSpec((TQ_SW, width), lambda b, i: (b * nq + i, 0)),
        out_shape=jax.ShapeDtypeStruct((ROWS, width), BF16),
        scratch_shapes=[
            pltpu.VMEM((SW_KV_HEADS, 2, 2 * SW_WINDOW, SW_GROUP * SW_WINDOW), F32),
        ],
        compiler_params=pltpu.CompilerParams(
            dimension_semantics=("arbitrary", "arbitrary"), vmem_limit_bytes=VMEM_LIMIT),
        name="sw_attn",
    )(slopes, sink_rows, attn, attn, attn)


def _sigmoid(x):
    return 1.0 / (1.0 + jnp.exp(-x))


def _tail_kernel(yda_ref, ysw_ref, g_ref, x_ref, wda_ref, wsw_ref, wmix_ref, nw_ref,
                 wup_ref, cw_ref, cb_ref, wdn_ref, fw_ref, o_ref, carry_ref, *, final):
    i = pl.program_id(0)
    tm = TM_TAIL
    seq_start = (i % (SEQ // tm)) == 0
    sub = lax.broadcasted_iota(jnp.int32, (SUBLANES, FF_CH), 0)

    t_da = jnp.dot(yda_ref[...], wda_ref[...], preferred_element_type=F32)
    t_sw = jnp.dot(ysw_ref[...], wsw_ref[...], preferred_element_type=F32)
    g_da = g_ref[:, :D_MODEL].astype(F32)
    g_sw = g_ref[:, D_MODEL:].astype(F32)
    merged = _sigmoid(g_da) * t_da + _sigmoid(g_sw) * t_sw
    x_mid = x_ref[...] + jnp.dot(merged.astype(BF16), wmix_ref[...], preferred_element_type=F32)
    o_ref[...] = x_mid
    xn = _rms(x_mid, nw_ref[...]).astype(BF16)

    def shifted(u, prev, k):
        rolled = pltpu.roll(u, k, 0)
        head = jnp.where(sub < k, pltpu.roll(prev, k, 0), rolled[:SUBLANES])
        return jnp.concatenate([head, rolled[SUBLANES:]], axis=0)

    def conv_chunk(col0):
        sl = slice(col0, col0 + FF_CH)
        u = jnp.dot(xn, wup_ref[:, sl], preferred_element_type=F32)
        prev = jnp.where(seq_start, 0.0, carry_ref[:, sl])
        carry_ref[:, sl] = u[tm - SUBLANES:, :]
        return (cb_ref[:, sl]
                + shifted(u, prev, 2) * cw_ref[0:1, sl]
                + shifted(u, prev, 1) * cw_ref[1:2, sl]
                + u * cw_ref[2:3, sl])

    h = []
    for c in range(N_FF_CH):
        gate = conv_chunk(c * FF_CH)
        val = conv_chunk(D_FF + c * FF_CH)
        h.append(((gate * _sigmoid(gate)) * val).astype(BF16))
    h = jnp.concatenate(h, axis=1)

    out = o_ref[...] + jnp.dot(h, wdn_ref[...], preferred_element_type=F32)
    if final:
        out = _rms(out, fw_ref[...])
    o_ref[...] = out


def _layer_tail(yda, ysw, gates, x, wda, wsw, wmix, nw, wup, cw, cb, wdn, fw, layer):
    half = DA_HEADS * 2 * DA_HEAD_DIM
    row = lambda i: (i, 0)
    fixed = lambda i: (0, 0)
    of_layer = lambda i: (layer, 0, 0)
    resident = dict(pipeline_mode=pl.Buffered(1))
    return pl.pallas_call(
        functools.partial(_tail_kernel, final=layer == DEPTH - 1),
        grid=(ROWS // TM_TAIL,),
        in_specs=[
            pl.BlockSpec((TM_TAIL, half), row),
            pl.BlockSpec((TM_TAIL, half), row),
            pl.BlockSpec((TM_TAIL, GATE_COLS), row),
            pl.BlockSpec((TM_TAIL, D_MODEL), row),
            pl.BlockSpec((None, half, D_MODEL), of_layer, **resident),
            pl.BlockSpec((None, half, D_MODEL), of_layer, **resident),
            pl.BlockSpec((None, D_MODEL, D_MODEL), of_layer, **resident),
            pl.BlockSpec((1, D_MODEL), fixed),
            pl.BlockSpec((None, D_MODEL, 2 * D_FF), of_layer, **resident),
            pl.BlockSpec((CONV_WIDTH, 2 * D_FF), fixed),
            pl.BlockSpec((1, 2 * D_FF), fixed),
            pl.BlockSpec((None, D_FF, D_MODEL), of_layer, **resident),
            pl.BlockSpec((1, D_MODEL), fixed),
        ],
        out_specs=pl.BlockSpec((TM_TAIL, D_MODEL), row),
        out_shape=jax.ShapeDtypeStruct((ROWS, D_MODEL), F32),
        scratch_shapes=[
            pltpu.VMEM((SUBLANES, 2 * D_FF), F32),
        ],
        compiler_params=pltpu.CompilerParams(
            dimension_semantics=("arbitrary",), vmem_limit_bytes=VMEM_LIMIT),
        name="layer_tail",
    )(yda, ysw, gates, x, wda, wsw, wmix, nw, wup, cw, cb, wdn, fw)


def _alibi_slopes(n_heads):
    hh = jnp.arange(1, n_heads + 1, dtype=F32)
    return jnp.exp2(-8.0 * hh / n_heads)


def kernel(x, norm_mix_w, w_in, lambda_q1, lambda_k1, lambda_q2, lambda_k2, subln_w, sinks,
           w_br_da, w_br_sw, w_mix_out, norm_ffn_w, w_up, conv_w, conv_b, w_down, norm_final_w):
    xf = x.reshape(ROWS, D_MODEL).astype(F32)
    w_in_b = w_in.astype(BF16)
    w_da_b = w_br_da.astype(BF16)
    w_sw_b = w_br_sw.astype(BF16)
    w_mix_b = w_mix_out.astype(BF16)
    w_up_b = w_up.astype(BF16)
    w_dn_b = w_down.astype(BF16)
    da_slopes = _alibi_slopes(DA_HEADS)
    sw_slopes = _alibi_slopes(SW_HEADS)

    for l in range(DEPTH):
        lam_init = 0.8 - 0.6 * math.exp(-0.3 * l)
        attn, gates = _inproj(xf, norm_mix_w[l].reshape(1, D_MODEL).astype(F32), w_in_b, l)
        da_scal = jnp.concatenate([da_slopes, jnp.full((1,), lam_init, F32)])
        vec = lambda a: a[l].reshape(1, DA_HEAD_DIM).astype(F32)
        y_da = _da_attention(attn, da_scal, vec(lambda_q1), vec(lambda_k1), vec(lambda_q2),
                             vec(lambda_k2), subln_w[l].reshape(2 * DA_HEAD_DIM, 1).astype(F32))
        sink_cols = sinks[l].astype(F32).reshape(SW_KV_HEADS, SW_GROUP)[:, list(SW_COL_ORDER)]
        sink_rows = jnp.repeat(sink_cols, SW_WINDOW, axis=1).reshape(
            SW_KV_HEADS, 1, SW_GROUP * SW_WINDOW)
        y_sw = _sw_attention(attn, sw_slopes, sink_rows)
        xf = _layer_tail(y_da, y_sw, gates, xf, w_da_b, w_sw_b, w_mix_b,
                         norm_ffn_w[l].reshape(1, D_MODEL).astype(F32), w_up_b,
                         conv_w[l].astype(F32), conv_b[l].reshape(1, 2 * D_FF).astype(F32),
                         w_dn_b, norm_final_w.reshape(1, D_MODEL).astype(F32), l)
    return xf.reshape(BATCH, SEQ, D_MODEL)
```

```python
import functools
import math

import jax
import jax.numpy as jnp
import numpy as np
from jax import lax
from jax.experimental import pallas as pl
from jax.experimental.pallas import tpu as pltpu

D_MODEL = 1024
BATCH = 4
SEQ = 4096
DEPTH = 4
ROWS = BATCH * SEQ

DA_HEADS = 4
DA_HEAD_DIM = 64
SW_HEADS = 8
SW_KV_HEADS = 2
SW_HEAD_DIM = 64
SW_GROUP = SW_HEADS // SW_KV_HEADS
SW_WINDOW = 128
D_FF = 2816
CONV_WIDTH = 3
EPS = 1e-6
NEG_INF = -1e30

ATTN_COLS = 2304
GATE_COLS = 2 * D_MODEL
IN_COLS = ATTN_COLS + GATE_COLS
LANES = 128
SUBLANES = 8
DA_K_BLK = 4
DA_V_BLK = 8
SW_Q_BLK512 = 3
SW_K_BLK = 16
SW_V_BLK = 17

VMEM_LIMIT = 56 * 1024 * 1024

BF16 = jnp.bfloat16
F32 = jnp.float32

TM_PROJ = 512
PROJ_CH = 256
TQ_DA = 1024
DA_TPS = SEQ // TQ_DA
TK_DA = 512
DA_VT_PAD = 16
DA_NT = 256
DA_POS_SHIFT = 8
TQ_SW = 512
SW_COL_ORDER = (0, 2, 1, 3)
TM_TAIL = 512
FF_CH = 256
N_FF_CH = D_FF // FF_CH


def _bf16_terms(x, n):
    terms, rest = [], float(x)
    for _ in range(n):
        t = float(np.asarray(rest, dtype=BF16).astype(np.float64))
        terms.append(t)
        rest -= t
    assert rest == 0.0, (x, terms)
    return terms


LOG2E = float(np.float32(1.4426950408889634))
LOG2E_TERMS = _bf16_terms(LOG2E, 3)
assert DA_HEAD_DIM == SW_HEAD_DIM
Q_SCALE = LOG2E * DA_HEAD_DIM ** -0.5
Q_COL_RANGES = ((0, DA_HEADS * 2 * DA_HEAD_DIM),
                (SW_Q_BLK512 * 512, SW_Q_BLK512 * 512 + SW_HEADS * SW_HEAD_DIM))


def _rms(x, w):
    ms = jnp.mean(x * x, axis=-1, keepdims=True)
    return (x * lax.rsqrt(ms + EPS)) * w


def _inproj_kernel(x_ref, nw_ref, w_ref, attn_ref, gate_ref):
    xn = _rms(x_ref[...], nw_ref[...]).astype(BF16)
    for c in range(ATTN_COLS // PROJ_CH):
        sl = slice(c * PROJ_CH, (c + 1) * PROJ_CH)
        res = jnp.dot(xn, w_ref[:, sl], preferred_element_type=F32)
        if any(lo <= c * PROJ_CH and (c + 1) * PROJ_CH <= hi for lo, hi in Q_COL_RANGES):
            res = res * Q_SCALE
        attn_ref[:, sl] = res.astype(BF16)
    for c in range(GATE_COLS // PROJ_CH):
        src = slice(ATTN_COLS + c * PROJ_CH, ATTN_COLS + (c + 1) * PROJ_CH)
        dst = slice(c * PROJ_CH, (c + 1) * PROJ_CH)
        gate_ref[:, dst] = jnp.dot(xn, w_ref[:, src], preferred_element_type=F32).astype(BF16)


def _inproj(x, nw, w, layer):
    return pl.pallas_call(
        _inproj_kernel,
        grid=(ROWS // TM_PROJ,),
        in_specs=[
            pl.BlockSpec((TM_PROJ, D_MODEL), lambda i: (i, 0)),
            pl.BlockSpec((1, D_MODEL), lambda i: (0, 0)),
            pl.BlockSpec((None, D_MODEL, IN_COLS), lambda i: (layer, 0, 0)),
        ],
        out_specs=[
            pl.BlockSpec((TM_PROJ, ATTN_COLS), lambda i: (i, 0)),
            pl.BlockSpec((TM_PROJ, GATE_COLS), lambda i: (i, 0)),
        ],
        out_shape=[
            jax.ShapeDtypeStruct((ROWS, ATTN_COLS), BF16),
            jax.ShapeDtypeStruct((ROWS, GATE_COLS), BF16),
        ],
        compiler_params=pltpu.CompilerParams(
            dimension_semantics=("arbitrary",), vmem_limit_bytes=VMEM_LIMIT),
        name="inproj",
    )(x, nw, w)


def _da_kernel(scal_ref, lq1_ref, lk1_ref, lq2_ref, lk2_ref, subw_ref, q_ref, k_ref, v_ref,
               o_ref, pos_ref, mask_ref, vt_ref, qs_ref, sa_ref, sb_ref, sa_max_ref, sb_max_ref,
               acc_ref, m_ref, l_ref):
    b = pl.program_id(0)
    h = pl.program_id(1)
    tq, tk, nt = TQ_DA, TK_DA, DA_NT
    tiles_per_map = tq // nt
    nt_dims = (((1,), (1,)), ((), ()))
    n_terms = len(LOG2E_TERMS)

    def build_value_rows():
        sub = lax.broadcasted_iota(jnp.int32, (DA_VT_PAD, tk), 0)
        ones_rows = jnp.where(sub == 0, 1.0, 0.0).astype(BF16)
        for jb in range(SEQ // tk):
            v_blk = v_ref[jb * tk:(jb + 1) * tk, :].astype(F32)
            vt_ref[jb, :2 * DA_HEAD_DIM, :] = v_blk.T.astype(BF16)
            vt_ref[jb, 2 * DA_HEAD_DIM:, :] = ones_rows

    @pl.when((b == 0) & (h == 0))
    def _():
        row = lax.broadcasted_iota(jnp.int32, (SEQ, LANES), 0)
        lane = lax.broadcasted_iota(jnp.int32, (SEQ, LANES), 1)
        hi = (row >> DA_POS_SHIFT).astype(F32)
        lo = (row & ((1 << DA_POS_SHIFT) - 1)).astype(F32)
        pos = jnp.where(lane < 2 * n_terms, jnp.where((lane & 1) == 0, hi, lo), 0.0)
        pos_ref[...] = pos.astype(BF16)
        kk = lax.broadcasted_iota(jnp.int32, (nt, nt), 0)
        qq = lax.broadcasted_iota(jnp.int32, (nt, nt), 1)
        mask_ref[...] = jnp.where(kk <= qq, 0.0, NEG_INF)

    slope = jnp.full((1, 1), scal_ref[h], F32)
    lane = lax.broadcasted_iota(jnp.int32, (tq, LANES), 1)
    slope_blk = jnp.zeros((tq, LANES), F32)
    for t, term in enumerate(LOG2E_TERMS):
        slope_blk = jnp.where(lane == 2 * t, slope * (term * (1 << DA_POS_SHIFT)), slope_blk)
        slope_blk = jnp.where(lane == 2 * t + 1, slope * term, slope_blk)
    slope_blk = slope_blk.astype(BF16)

    def build_queries(k):
        q = q_ref[k * tq:(k + 1) * tq, :]
        zero = jnp.zeros_like(q)
        qs_ref[k, :tq, :LANES] = jnp.where(lane < DA_HEAD_DIM, q, zero)
        qs_ref[k, tq:, :LANES] = jnp.where(lane >= DA_HEAD_DIM, q, zero)
        qs_ref[k, :tq, LANES:] = slope_blk
        qs_ref[k, tq:, LANES:] = slope_blk

    def reset_state():
        m_ref[...] = jnp.full(m_ref.shape, NEG_INF, F32)
        l_ref[...] = jnp.zeros(l_ref.shape, F32)
        acc_ref[...] = jnp.zeros(acc_ref.shape, F32)

    n_cols = 2 * tiles_per_map
    halves = (range(n_cols // 2), range(n_cols // 2, n_cols))

    subs = tk // nt

    def visible(c, first_sub):
        if first_sub is None:
            return subs, False
        diag = c % tiles_per_map - first_sub
        return max(0, min(subs, diag + 1)), 0 <= diag < subs

    def scores(k, j, buf, tiles=range(n_cols), first_sub=None):
        s_ref, smax_ref = buf
        keys = pl.ds(pl.multiple_of(j * tk, tk), tk) if isinstance(j, jax.Array) \
            else slice(j * tk, (j + 1) * tk)
        kx = jnp.concatenate([k_ref[keys, :], pos_ref[keys, :]], axis=1)
        for c in tiles:
            if visible(c, first_sub)[0] == 0:
                continue
            cols = slice(c * nt, (c + 1) * nt)
            s = lax.dot_general(kx, qs_ref[k, cols, :], nt_dims,
                                preferred_element_type=F32)
            s_ref[:, cols] = s
            smax_ref[:, cols] = jnp.max(s, axis=0, keepdims=True)

    def softmax_pv(j, buf, first_sub, tiles=range(n_cols)):
        s_ref, smax_ref = buf
        for c in tiles:
            n_sub, causal = visible(c, first_sub)
            if n_sub == 0:
                continue
            cols = slice(c * nt, (c + 1) * nt)
            rows = n_sub * nt
            s = s_ref[:rows, cols]
            if causal and rows == nt:
                s = s + mask_ref[...]
            elif causal:
                s = jnp.concatenate([s[:rows - nt], s[rows - nt:] + mask_ref[...]], axis=0)
            s_max = smax_ref[:, cols] if first_sub is None else jnp.max(s, axis=0, keepdims=True)
            m_old = m_ref[:, cols]
            m_new = jnp.maximum(m_old, s_max)
            alpha = jnp.exp2(m_old - m_new)
            p = jnp.exp2(s - m_new).astype(BF16)
            pv = jnp.dot(vt_ref[j, :, :rows], p, preferred_element_type=F32)
            acc_ref[:, cols] = alpha * acc_ref[:, cols] + pv[:2 * DA_HEAD_DIM]
            l_ref[:, cols] = alpha * l_ref[:, cols] + pv[2 * DA_HEAD_DIM:2 * DA_HEAD_DIM + 1]
            m_ref[:, cols] = m_new

    def overlapped(k_next, j_next, next_ref, j_cur, cur_ref, next_sub=None, cur_sub=None):
        for half in halves:
            scores(k_next, j_next, next_ref, half, next_sub)
            softmax_pv(j_cur, cur_ref, cur_sub, half)

    lam_init = jnp.full((1, 1), scal_ref[DA_HEADS], F32)
    lam = (jnp.exp(jnp.sum(lq1_ref[...] * lk1_ref[...], axis=-1, keepdims=True))
           - jnp.exp(jnp.sum(lq2_ref[...] * lk2_ref[...], axis=-1, keepdims=True)) + lam_init)

    def finish(k):
        inv = 1.0 / l_ref[...]
        acc = acc_ref[...]
        o = acc[:, :tq] * inv[:, :tq] - lam * (acc[:, tq:] * inv[:, tq:])
        ms = jnp.mean(o * o, axis=0, keepdims=True)
        y = ((o * lax.rsqrt(ms + EPS)) * subw_ref[...]) * (1.0 - lam_init)
        o_ref[k * tq:(k + 1) * tq, :] = y.T.astype(BF16)

    assert tq == 2 * tk
    buf_a, buf_b = (sa_ref, sa_max_ref), (sb_ref, sb_max_ref)
    build_queries(0)
    reset_state()
    scores(0, 0, buf_a, first_sub=0)
    build_value_rows()
    for k in range(1, DA_TPS):
        build_queries(k)
    for k in range(DA_TPS):

        def pair(t, carry, k=k):
            overlapped(k, 2 * t + 1, buf_b, 2 * t, buf_a)
            overlapped(k, 2 * t + 2, buf_a, 2 * t + 1, buf_b)
            return carry

        lax.fori_loop(0, k, pair, 0)
        overlapped(k, 2 * k + 1, buf_b, 2 * k, buf_a, next_sub=subs, cur_sub=0)
        if k + 1 < DA_TPS:
            overlapped(k + 1, 0, buf_a, 2 * k + 1, buf_b, cur_sub=subs)
            finish(k)
            reset_state()
        else:
            softmax_pv(2 * k + 1, buf_b, subs)
            finish(k)


def _da_attention(attn, scal, lq1, lk1, lq2, lk2, subw):
    vec = pl.BlockSpec((1, DA_HEAD_DIM), lambda b, h: (0, 0))
    return pl.pallas_call(
        _da_kernel,
        grid=(BATCH, DA_HEADS),
        in_specs=[
            pl.BlockSpec(memory_space=pltpu.SMEM),
            vec, vec, vec, vec,
            pl.BlockSpec((2 * DA_HEAD_DIM, 1), lambda b, h: (0, 0)),
            pl.BlockSpec((SEQ, LANES), lambda b, h: (b, h)),
            pl.BlockSpec((SEQ, LANES), lambda b, h: (b, DA_K_BLK + h)),
            pl.BlockSpec((SEQ, LANES), lambda b, h: (b, DA_V_BLK + h)),
        ],
        out_specs=pl.BlockSpec((SEQ, LANES), lambda b, h: (b, h)),
        out_shape=jax.ShapeDtypeStruct((ROWS, DA_HEADS * 2 * DA_HEAD_DIM), BF16),
        scratch_shapes=[
            pltpu.VMEM((SEQ, LANES), BF16),
            pltpu.VMEM((DA_NT, DA_NT), F32),
            pltpu.VMEM((SEQ // TK_DA, 2 * DA_HEAD_DIM + DA_VT_PAD, TK_DA), BF16),
            pltpu.VMEM((DA_TPS, 2 * TQ_DA, 2 * LANES), BF16),
            pltpu.VMEM((TK_DA, 2 * TQ_DA), F32),
            pltpu.VMEM((TK_DA, 2 * TQ_DA), F32),
            pltpu.VMEM((1, 2 * TQ_DA), F32),
            pltpu.VMEM((1, 2 * TQ_DA), F32),
            pltpu.VMEM((2 * DA_HEAD_DIM, 2 * TQ_DA), F32),
            pltpu.VMEM((1, 2 * TQ_DA), F32),
            pltpu.VMEM((1, 2 * TQ_DA), F32),
        ],
        compiler_params=pltpu.CompilerParams(
            dimension_semantics=("arbitrary", "arbitrary"),
            vmem_limit_bytes=VMEM_LIMIT),
        name="diff_attn",
    )(scal, lq1, lk1, lq2, lk2, subw, attn, attn, attn)


def _sw_kernel(slope_ref, sink_ref, q_ref, k_ref, v_ref, o_ref, tab_ref):
    b = pl.program_id(0)
    i = pl.program_id(1)
    w = SW_WINDOW
    d = SW_HEAD_DIM
    ncol = SW_GROUP * w

    @pl.when((b == 0) & (i == 0))
    def _():
        kk = lax.broadcasted_iota(jnp.int32, (2 * w, ncol), 0)
        col = lax.broadcasted_iota(jnp.int32, (2 * w, ncol), 1)
        qq = col & (w - 1)
        grp = col >> 7
        for c in range(SW_KV_HEADS):
            slope = jnp.zeros((2 * w, ncol), F32)
            for pos, g in enumerate(SW_COL_ORDER):
                slope = jnp.where(grp == pos, slope_ref[c * SW_GROUP + g], slope)
            slope = slope * LOG2E
            dist = qq + w - kk
            tab_ref[c, 0] = jnp.where((dist >= 0) & (dist < w), -slope * dist.astype(F32), NEG_INF)
            dist0 = qq - kk
            tab_ref[c, 1] = jnp.where(dist0 >= 0, -slope * dist0.astype(F32), NEG_INF)

    lane = lax.broadcasted_iota(jnp.int32, (w, LANES), 1)
    n_blk = TQ_SW // w
    pad_row = lax.broadcasted_iota(jnp.int32, (DA_VT_PAD, 2 * w), 0)
    ones_rows = jnp.where(pad_row == 0, 1.0, 0.0).astype(BF16)

    def window(bb):
        n = i * n_blk + bb
        if bb == 0:
            return pl.multiple_of(jnp.maximum(n - 1, 0) * w, w), jnp.where(n == 0, 1, 0)
        return pl.multiple_of((n - 1) * w, w), 0

    def scores(bb):
        start, tsel = window(bb)
        kwin = k_ref[pl.ds(start, 2 * w), :]
        kswp = jnp.concatenate([kwin[:, d:], kwin[:, :d]], axis=1)
        qblk = q_ref[bb * w:(bb + 1) * w, :]
        out = []
        for c in range(SW_KV_HEADS):
            s_parts = []
            for pair in (SW_COL_ORDER[:2], SW_COL_ORDER[2:]):
                qz = []
                for g in pair:
                    hh = c * SW_GROUP + g
                    qpair = qblk[:, (hh // 2) * LANES:(hh // 2 + 1) * LANES]
                    keep = (lane >= d) if hh % 2 else (lane < d)
                    qz.append(jnp.where(keep, qpair, jnp.zeros_like(qpair)))
                kk_ = kwin if pair[0] % 2 == c else kswp
                s_parts.append(lax.dot_general(kk_, jnp.concatenate(qz, axis=0),
                                               (((1,), (1,)), ((), ())),
                                               preferred_element_type=F32))
            out.append(jnp.concatenate(s_parts, axis=1) + tab_ref[c, tsel])
        return out

    def softmax_pv(bb, u_all):
        start, _ = window(bb)
        v_t = v_ref[pl.ds(start, 2 * w), :].astype(F32).T.astype(BF16)
        o_rows = []
        for c in range(SW_KV_HEADS):
            u = u_all[c]
            sink = sink_ref[c] * LOG2E
            m = jnp.maximum(jnp.max(u, axis=0, keepdims=True), sink)
            p = jnp.exp2(u - m).astype(BF16)
            v_ext = jnp.concatenate([v_t[c * d:(c + 1) * d, :], ones_rows], axis=0)
            pv = jnp.dot(v_ext, p, preferred_element_type=F32)
            inv = 1.0 / (pv[d:d + 1, :] + jnp.exp2(sink - m))
            o_c = pv[:d, :] * inv
            for g in range(SW_GROUP):
                col = SW_COL_ORDER.index(g) * w
                o_rows.append(o_c[:, col:col + w])
        o_t = jnp.concatenate(o_rows, axis=0)
        o_ref[bb * w:(bb + 1) * w, :] = o_t.T.astype(BF16)

    u_next = scores(0)
    for bb in range(n_blk):
        u_cur = u_next
        if bb + 1 < n_blk:
            u_next = scores(bb + 1)
        softmax_pv(bb, u_cur)


def _sw_attention(attn, slopes, sink_rows):
    nq = SEQ // TQ_SW
    width = SW_HEADS * SW_HEAD_DIM
    return pl.pallas_call(
        _sw_kernel,
        grid=(BATCH, nq),
        in_specs=[
            pl.BlockSpec(memory_space=pltpu.SMEM),
            pl.BlockSpec((SW_KV_HEADS, 1, SW_GROUP * SW_WINDOW), lambda b, i: (0, 0, 0)),
            pl.BlockSpec((TQ_SW, width), lambda b, i: (b * nq + i, SW_Q_BLK512)),
            pl.BlockSpec((SEQ, LANES), lambda b, i: (b, SW_K_BLK)),
            pl.BlockSpec((SEQ, LANES), lambda b, i: (b, SW_V_BLK)),
        ],
        out_specs=pl.BlockSpec((TQ_SW, width), lambda b, i: (b * nq + i, 0)),
        out_shape=jax.ShapeDtypeStruct((ROWS, width), BF16),
        scratch_shapes=[
            pltpu.VMEM((SW_KV_HEADS, 2, 2 * SW_WINDOW, SW_GROUP * SW_WINDOW), F32),
        ],
        compiler_params=pltpu.CompilerParams(
            dimension_semantics=("arbitrary", "arbitrary"), vmem_limit_bytes=VMEM_LIMIT),
        name="sw_attn",
    )(slopes, sink_rows, attn, attn, attn)


def _sigmoid(x):
    return 1.0 / (1.0 + jnp.exp(-x))


def _tail_kernel(yda_ref, ysw_ref, g_ref, x_ref, wda_ref, wsw_ref, wmix_ref, nw_ref,
                 wup_ref, cw_ref, cb_ref, wdn_ref, fw_ref, o_ref, carry_ref, *, final):
    i = pl.program_id(0)
    tm = TM_TAIL
    seq_start = (i % (SEQ // tm)) == 0
    sub = lax.broadcasted_iota(jnp.int32, (SUBLANES, FF_CH), 0)

    t_da = jnp.dot(yda_ref[...], wda_ref[...], preferred_element_type=F32)
    t_sw = jnp.dot(ysw_ref[...], wsw_ref[...], preferred_element_type=F32)
    g_da = g_ref[:, :D_MODEL].astype(F32)
    g_sw = g_ref[:, D_MODEL:].astype(F32)
    merged = _sigmoid(g_da) * t_da + _sigmoid(g_sw) * t_sw
    x_mid = x_ref[...] + jnp.dot(merged.astype(BF16), wmix_ref[...], preferred_element_type=F32)
    o_ref[...] = x_mid
    xn = _rms(x_mid, nw_ref[...]).astype(BF16)

    def shifted(u, prev, k):
        rolled = pltpu.roll(u, k, 0)
        head = jnp.where(sub < k, pltpu.roll(prev, k, 0), rolled[:SUBLANES])
        return jnp.concatenate([head, rolled[SUBLANES:]], axis=0)

    def conv_chunk(col0):
        sl = slice(col0, col0 + FF_CH)
        u = jnp.dot(xn, wup_ref[:, sl], preferred_element_type=F32)
        prev = jnp.where(seq_start, 0.0, carry_ref[:, sl])
        carry_ref[:, sl] = u[tm - SUBLANES:, :]
        return (cb_ref[:, sl]
                + shifted(u, prev, 2) * cw_ref[0:1, sl]
                + shifted(u, prev, 1) * cw_ref[1:2, sl]
                + u * cw_ref[2:3, sl])

    h = []
    for c in range(N_FF_CH):
        gate = conv_chunk(c * FF_CH)
        val = conv_chunk(D_FF + c * FF_CH)
        h.append(((gate * _sigmoid(gate)) * val).astype(BF16))
    h = jnp.concatenate(h, axis=1)

    out = o_ref[...] + jnp.dot(h, wdn_ref[...], preferred_element_type=F32)
    if final:
        out = _rms(out, fw_ref[...])
    o_ref[...] = out


def _layer_tail(yda, ysw, gates, x, wda, wsw, wmix, nw, wup, cw, cb, wdn, fw, layer):
    half = DA_HEADS * 2 * DA_HEAD_DIM
    row = lambda i: (i, 0)
    fixed = lambda i: (0, 0)
    of_layer = lambda i: (layer, 0, 0)
    resident = dict(pipeline_mode=pl.Buffered(1))
    return pl.pallas_call(
        functools.partial(_tail_kernel, final=layer == DEPTH - 1),
        grid=(ROWS // TM_TAIL,),
        in_specs=[
            pl.BlockSpec((TM_TAIL, half), row),
            pl.BlockSpec((TM_TAIL, half), row),
            pl.BlockSpec((TM_TAIL, GATE_COLS), row),
            pl.BlockSpec((TM_TAIL, D_MODEL), row),
            pl.BlockSpec((None, half, D_MODEL), of_layer, **resident),
            pl.BlockSpec((None, half, D_MODEL), of_layer, **resident),
            pl.BlockSpec((None, D_MODEL, D_MODEL), of_layer, **resident),
            pl.BlockSpec((1, D_MODEL), fixed),
            pl.BlockSpec((None, D_MODEL, 2 * D_FF), of_layer, **resident),
            pl.BlockSpec((CONV_WIDTH, 2 * D_FF), fixed),
            pl.BlockSpec((1, 2 * D_FF), fixed),
            pl.BlockSpec((None, D_FF, D_MODEL), of_layer, **resident),
            pl.BlockSpec((1, D_MODEL), fixed),
        ],
        out_specs=pl.BlockSpec((TM_TAIL, D_MODEL), row),
        out_shape=jax.ShapeDtypeStruct((ROWS, D_MODEL), F32),
        scratch_shapes=[
            pltpu.VMEM((SUBLANES, 2 * D_FF), F32),
        ],
        compiler_params=pltpu.CompilerParams(
            dimension_semantics=("arbitrary",), vmem_limit_bytes=VMEM_LIMIT),
        name="layer_tail",
    )(yda, ysw, gates, x, wda, wsw, wmix, nw, wup, cw, cb, wdn, fw)


def _alibi_slopes(n_heads):
    hh = jnp.arange(1, n_heads + 1, dtype=F32)
    return jnp.exp2(-8.0 * hh / n_heads)


def kernel(x, norm_mix_w, w_in, lambda_q1, lambda_k1, lambda_q2, lambda_k2, subln_w, sinks,
           w_br_da, w_br_sw, w_mix_out, norm_ffn_w, w_up, conv_w, conv_b, w_down, norm_final_w):
    xf = x.reshape(ROWS, D_MODEL).astype(F32)
    w_in_b = w_in.astype(BF16)
    w_da_b = w_br_da.astype(BF16)
    w_sw_b = w_br_sw.astype(BF16)
    w_mix_b = w_mix_out.astype(BF16)
    w_up_b = w_up.astype(BF16)
    w_dn_b = w_down.astype(BF16)
    da_slopes = _alibi_slopes(DA_HEADS)
    sw_slopes = _alibi_slopes(SW_HEADS)

    for l in range(DEPTH):
        lam_init = 0.8 - 0.6 * math.exp(-0.3 * l)
        attn, gates = _inproj(xf, norm_mix_w[l].reshape(1, D_MODEL).astype(F32), w_in_b, l)
        da_scal = jnp.concatenate([da_slopes, jnp.full((1,), lam_init, F32)])
        vec = lambda a: a[l].reshape(1, DA_HEAD_DIM).astype(F32)
        y_da = _da_attention(attn, da_scal, vec(lambda_q1), vec(lambda_k1), vec(lambda_q2),
                             vec(lambda_k2), subln_w[l].reshape(2 * DA_HEAD_DIM, 1).astype(F32))
        sink_cols = sinks[l].astype(F32).reshape(SW_KV_HEADS, SW_GROUP)[:, list(SW_COL_ORDER)]
        sink_rows = jnp.repeat(sink_cols, SW_WINDOW, axis=1).reshape(
            SW_KV_HEADS, 1, SW_GROUP * SW_WINDOW)
        y_sw = _sw_attention(attn, sw_slopes, sink_rows)
        xf = _layer_tail(y_da, y_sw, gates, xf, w_da_b, w_sw_b, w_mix_b,
                         norm_ffn_w[l].reshape(1, D_MODEL).astype(F32), w_up_b,
                         conv_w[l].astype(F32), conv_b[l].reshape(1, 2 * D_FF).astype(F32),
                         w_dn_b, norm_final_w.reshape(1, D_MODEL).astype(F32), l)
    return xf.reshape(BATCH, SEQ, D_MODEL)
```

```python
import functools
import math

import jax
import jax.numpy as jnp
import numpy as np
from jax import lax
from jax.experimental import pallas as pl
from jax.experimental.pallas import tpu as pltpu

D_MODEL = 1024
BATCH = 4
SEQ = 4096
DEPTH = 4
ROWS = BATCH * SEQ

DA_HEADS = 4
DA_HEAD_DIM = 64
SW_HEADS = 8
SW_KV_HEADS = 2
SW_HEAD_DIM = 64
SW_GROUP = SW_HEADS // SW_KV_HEADS
SW_WINDOW = 128
D_FF = 2816
CONV_WIDTH = 3
EPS = 1e-6
NEG_INF = -1e30

ATTN_COLS = 2304
GATE_COLS = 2 * D_MODEL
IN_COLS = ATTN_COLS + GATE_COLS
LANES = 128
SUBLANES = 8
DA_K_BLK = 4
DA_V_BLK = 8
SW_Q_BLK512 = 3
SW_K_BLK = 16
SW_V_BLK = 17

VMEM_LIMIT = 56 * 1024 * 1024

BF16 = jnp.bfloat16
F32 = jnp.float32

TM_PROJ = 1024
PROJ_CH = 256
TQ_DA = 1024
DA_TPS = SEQ // TQ_DA
DA_HPS = 1
TK_DA = 512
DA_VT_PAD = 16
DA_NT = 256
DA_POS_SHIFT = 8
TQ_SW = 512
SW_COL_ORDER = (0, 2, 1, 3)
TM_TAIL = 512
FF_CH = 256
N_FF_CH = D_FF // FF_CH


def _bf16_terms(x, n):
    terms, rest = [], float(x)
    for _ in range(n):
        t = float(np.asarray(rest, dtype=BF16).astype(np.float64))
        terms.append(t)
        rest -= t
    assert rest == 0.0, (x, terms)
    return terms


LOG2E = float(np.float32(1.4426950408889634))
LOG2E_TERMS = _bf16_terms(LOG2E, 3)
assert DA_HEAD_DIM == SW_HEAD_DIM
Q_SCALE = LOG2E * DA_HEAD_DIM ** -0.5
Q_COL_RANGES = ((0, DA_HEADS * 2 * DA_HEAD_DIM),
                (SW_Q_BLK512 * 512, SW_Q_BLK512 * 512 + SW_HEADS * SW_HEAD_DIM))


def _rms(x, w):
    ms = jnp.mean(x * x, axis=-1, keepdims=True)
    return (x * lax.rsqrt(ms + EPS)) * w


def _inproj_kernel(x_ref, nw_ref, w_ref, attn_ref, gate_ref):
    xn = _rms(x_ref[...], nw_ref[...]).astype(BF16)
    for c in range(ATTN_COLS // PROJ_CH):
        sl = slice(c * PROJ_CH, (c + 1) * PROJ_CH)
        res = jnp.dot(xn, w_ref[:, sl], preferred_element_type=F32)
        if any(lo <= c * PROJ_CH and (c + 1) * PROJ_CH <= hi for lo, hi in Q_COL_RANGES):
            res = res * Q_SCALE
        attn_ref[:, sl] = res.astype(BF16)
    for c in range(GATE_COLS // PROJ_CH):
        src = slice(ATTN_COLS + c * PROJ_CH, ATTN_COLS + (c + 1) * PROJ_CH)
        dst = slice(c * PROJ_CH, (c + 1) * PROJ_CH)
        gate_ref[:, dst] = jnp.dot(xn, w_ref[:, src], preferred_element_type=F32).astype(BF16)


def _inproj(x, nw, w, layer):
    return pl.pallas_call(
        _inproj_kernel,
        grid=(ROWS // TM_PROJ,),
        in_specs=[
            pl.BlockSpec((TM_PROJ, D_MODEL), lambda i: (i, 0)),
            pl.BlockSpec((1, D_MODEL), lambda i: (0, 0)),
            pl.BlockSpec((None, D_MODEL, IN_COLS), lambda i: (layer, 0, 0),
                         pipeline_mode=pl.Buffered(1)),
        ],
        out_specs=[
            pl.BlockSpec((TM_PROJ, ATTN_COLS), lambda i: (i, 0)),
            pl.BlockSpec((TM_PROJ, GATE_COLS), lambda i: (i, 0)),
        ],
        out_shape=[
            jax.ShapeDtypeStruct((ROWS, ATTN_COLS), BF16),
            jax.ShapeDtypeStruct((ROWS, GATE_COLS), BF16),
        ],
        compiler_params=pltpu.CompilerParams(
            dimension_semantics=("arbitrary",), vmem_limit_bytes=VMEM_LIMIT),
        name="inproj",
    )(x, nw, w)


def _da_kernel(scal_ref, lq1_ref, lk1_ref, lq2_ref, lk2_ref, subw_ref, q_ref, k_ref, v_ref,
               o_ref, pos_ref, mask_ref, vt_ref, qs_ref, sa_ref, sb_ref, sa_max_ref, sb_max_ref,
               acc_ref, m_ref, l_ref):
    b = pl.program_id(0)
    hp = pl.program_id(1)
    tq, tk, nt = TQ_DA, TK_DA, DA_NT
    tiles_per_map = tq // nt
    nt_dims = (((1,), (1,)), ((), ()))
    n_terms = len(LOG2E_TERMS)
    head_lanes = lambda hh: slice(hh * LANES, (hh + 1) * LANES)

    def build_value_rows(hh):
        sub = lax.broadcasted_iota(jnp.int32, (DA_VT_PAD, tk), 0)
        ones_rows = jnp.where(sub == 0, 1.0, 0.0).astype(BF16)
        for jb in range(SEQ // tk):
            v_blk = v_ref[jb * tk:(jb + 1) * tk, head_lanes(hh)].astype(F32)
            vt_ref[hh, jb, :2 * DA_HEAD_DIM, :] = v_blk.T.astype(BF16)
            vt_ref[hh, jb, 2 * DA_HEAD_DIM:, :] = ones_rows

    @pl.when((b == 0) & (hp == 0))
    def _():
        row = lax.broadcasted_iota(jnp.int32, (SEQ, LANES), 0)
        lane = lax.broadcasted_iota(jnp.int32, (SEQ, LANES), 1)
        hi = (row >> DA_POS_SHIFT).astype(F32)
        lo = (row & ((1 << DA_POS_SHIFT) - 1)).astype(F32)
        pos = jnp.where(lane < 2 * n_terms, jnp.where((lane & 1) == 0, hi, lo), 0.0)
        pos_ref[...] = pos.astype(BF16)
        kk = lax.broadcasted_iota(jnp.int32, (nt, nt), 0)
        qq = lax.broadcasted_iota(jnp.int32, (nt, nt), 1)
        mask_ref[...] = jnp.where(kk <= qq, 0.0, NEG_INF)

    lane = lax.broadcasted_iota(jnp.int32, (tq, LANES), 1)

    def build_queries(n):
        hh, k = divmod(n, DA_TPS)
        slope = jnp.full((1, 1), scal_ref[hp * DA_HPS + hh], F32)
        slope_blk = jnp.zeros((tq, LANES), F32)
        for t, term in enumerate(LOG2E_TERMS):
            slope_blk = jnp.where(lane == 2 * t, slope * (term * (1 << DA_POS_SHIFT)), slope_blk)
            slope_blk = jnp.where(lane == 2 * t + 1, slope * term, slope_blk)
        slope_blk = slope_blk.astype(BF16)
        q = q_ref[k * tq:(k + 1) * tq, head_lanes(hh)]
        zero = jnp.zeros_like(q)
        qs_ref[n, :tq, :LANES] = jnp.where(lane < DA_HEAD_DIM, q, zero)
        qs_ref[n, tq:, :LANES] = jnp.where(lane >= DA_HEAD_DIM, q, zero)
        qs_ref[n, :tq, LANES:] = slope_blk
        qs_ref[n, tq:, LANES:] = slope_blk

    def reset_state(n):
        m_ref[n % 2] = jnp.full(m_ref.shape[1:], NEG_INF, F32)
        l_ref[n % 2] = jnp.zeros(l_ref.shape[1:], F32)
        acc_ref[n % 2] = jnp.zeros(acc_ref.shape[1:], F32)

    n_cols = 2 * tiles_per_map
    halves = (range(n_cols // 2), range(n_cols // 2, n_cols))

    subs = tk // nt

    def visible(c, first_sub):
        if first_sub is None:
            return subs, False
        diag = c % tiles_per_map - first_sub
        return max(0, min(subs, diag + 1)), 0 <= diag < subs

    def scores(n, j, buf, tiles=range(n_cols), first_sub=None):
        s_ref, smax_ref = buf
        hh = n // DA_TPS
        keys = pl.ds(pl.multiple_of(j * tk, tk), tk) if isinstance(j, jax.Array) \
            else slice(j * tk, (j + 1) * tk)
        kx = jnp.concatenate([k_ref[keys, head_lanes(hh)], pos_ref[keys, :]], axis=1)
        for c in tiles:
            if visible(c, first_sub)[0] == 0:
                continue
            cols = slice(c * nt, (c + 1) * nt)
            s = lax.dot_general(kx, qs_ref[n, cols, :], nt_dims,
                                preferred_element_type=F32)
            s_ref[:, cols] = s
            smax_ref[:, cols] = jnp.max(s, axis=0, keepdims=True)

    def softmax_pv(n, j, buf, first_sub, tiles=range(n_cols)):
        s_ref, smax_ref = buf
        hh, st = n // DA_TPS, n % 2
        for c in tiles:
            n_sub, causal = visible(c, first_sub)
            if n_sub == 0:
                continue
            cols = slice(c * nt, (c + 1) * nt)
            rows = n_sub * nt
            s = s_ref[:rows, cols]
            if causal and rows == nt:
                s = s + mask_ref[...]
            elif causal:
                s = jnp.concatenate([s[:rows - nt], s[rows - nt:] + mask_ref[...]], axis=0)
            s_max = smax_ref[:, cols] if first_sub is None else jnp.max(s, axis=0, keepdims=True)
            m_old = m_ref[st, :, cols]
            m_new = jnp.maximum(m_old, s_max)
            alpha = jnp.exp2(m_old - m_new)
            p = jnp.exp2(s - m_new).astype(BF16)
            pv = jnp.dot(vt_ref[hh, j, :, :rows], p, preferred_element_type=F32)
            acc_ref[st, :, cols] = alpha * acc_ref[st, :, cols] + pv[:2 * DA_HEAD_DIM]
            l_ref[st, :, cols] = (alpha * l_ref[st, :, cols]
                                  + pv[2 * DA_HEAD_DIM:2 * DA_HEAD_DIM + 1])
            m_ref[st, :, cols] = m_new

    def overlapped(n_next, j_next, next_ref, n_cur, j_cur, cur_ref, next_sub=None, cur_sub=None,
                   between=None):
        for idx, half in enumerate(halves):
            scores(n_next, j_next, next_ref, half, next_sub)
            if idx == 0 and between is not None:
                between()
            softmax_pv(n_cur, j_cur, cur_ref, cur_sub, half)

    lam_init = jnp.full((1, 1), scal_ref[DA_HEADS], F32)
    lam = (jnp.exp(jnp.sum(lq1_ref[...] * lk1_ref[...], axis=-1, keepdims=True))
           - jnp.exp(jnp.sum(lq2_ref[...] * lk2_ref[...], axis=-1, keepdims=True)) + lam_init)

    def finish(n):
        hh, k = divmod(n, DA_TPS)
        inv = 1.0 / l_ref[n % 2]
        acc = acc_ref[n % 2]
        o = acc[:, :tq] * inv[:, :tq] - lam * (acc[:, tq:] * inv[:, tq:])
        ms = jnp.mean(o * o, axis=0, keepdims=True)
        y = ((o * lax.rsqrt(ms + EPS)) * subw_ref[...]) * (1.0 - lam_init)
        o_ref[k * tq:(k + 1) * tq, head_lanes(hh)] = y.T.astype(BF16)

    assert tq == 2 * tk
    n_tiles = DA_HPS * DA_TPS
    buf_a, buf_b = (sa_ref, sa_max_ref), (sb_ref, sb_max_ref)
    build_queries(0)
    scores(0, 0, buf_a, first_sub=0)
    for hh in range(DA_HPS):
        build_value_rows(hh)
    for n in range(1, n_tiles):
        build_queries(n)
    for n in range(n_tiles):
        k = n % DA_TPS
        reset_state(n)
        finish_prev = functools.partial(finish, n - 1) if n > 0 else None
        if k == 0:
            overlapped(n, 1, buf_b, n, 0, buf_a, next_sub=subs, cur_sub=0, between=finish_prev)
        else:
            overlapped(n, 1, buf_b, n, 0, buf_a, between=finish_prev)

            def pair(t, carry, n=n):
                overlapped(n, 2 * t + 2, buf_a, n, 2 * t + 1, buf_b)
                overlapped(n, 2 * t + 3, buf_b, n, 2 * t + 2, buf_a)
                return carry

            lax.fori_loop(0, k - 1, pair, 0)
            overlapped(n, 2 * k, buf_a, n, 2 * k - 1, buf_b)
            overlapped(n, 2 * k + 1, buf_b, n, 2 * k, buf_a, next_sub=subs, cur_sub=0)
        if n + 1 < n_tiles:
            overlapped(n + 1, 0, buf_a, n, 2 * k + 1, buf_b, cur_sub=subs,
                       next_sub=0 if (n + 1) % DA_TPS == 0 else None)
        else:
            softmax_pv(n, 2 * k + 1, buf_b, subs)
            finish(n)


def _da_attention(attn, scal, lq1, lk1, lq2, lk2, subw):
    vec = pl.BlockSpec((1, DA_HEAD_DIM), lambda b, h: (0, 0))
    width = DA_HPS * LANES
    k_blk, v_blk = DA_K_BLK // DA_HPS, DA_V_BLK // DA_HPS
    return pl.pallas_call(
        _da_kernel,
        grid=(BATCH, DA_HEADS // DA_HPS),
        in_specs=[
            pl.BlockSpec(memory_space=pltpu.SMEM),
            vec, vec, vec, vec,
            pl.BlockSpec((2 * DA_HEAD_DIM, 1), lambda b, h: (0, 0)),
            pl.BlockSpec((SEQ, width), lambda b, h: (b, h)),
            pl.BlockSpec((SEQ, width), lambda b, h: (b, k_blk + h)),
            pl.BlockSpec((SEQ, width), lambda b, h: (b, v_blk + h)),
        ],
        out_specs=pl.BlockSpec((SEQ, width), lambda b, h: (b, h)),
        out_shape=jax.ShapeDtypeStruct((ROWS, DA_HEADS * 2 * DA_HEAD_DIM), BF16),
        scratch_shapes=[
            pltpu.VMEM((SEQ, LANES), BF16),
            pltpu.VMEM((DA_NT, DA_NT), F32),
            pltpu.VMEM((DA_HPS, SEQ // TK_DA, 2 * DA_HEAD_DIM + DA_VT_PAD, TK_DA), BF16),
            pltpu.VMEM((DA_HPS * DA_TPS, 2 * TQ_DA, 2 * LANES), BF16),
            pltpu.VMEM((TK_DA, 2 * TQ_DA), F32),
            pltpu.VMEM((TK_DA, 2 * TQ_DA), F32),
            pltpu.VMEM((1, 2 * TQ_DA), F32),
            pltpu.VMEM((1, 2 * TQ_DA), F32),
            pltpu.VMEM((2, 2 * DA_HEAD_DIM, 2 * TQ_DA), F32),
            pltpu.VMEM((2, 1, 2 * TQ_DA), F32),
            pltpu.VMEM((2, 1, 2 * TQ_DA), F32),
        ],
        compiler_params=pltpu.CompilerParams(
            dimension_semantics=("arbitrary", "arbitrary"),
            vmem_limit_bytes=VMEM_LIMIT),
        name="diff_attn",
    )(scal, lq1, lk1, lq2, lk2, subw, attn, attn, attn)


def _sw_kernel(slope_ref, sink_ref, q_ref, k_ref, v_ref, o_ref, tab_ref):
    b = pl.program_id(0)
    i = pl.program_id(1)
    w = SW_WINDOW
    d = SW_HEAD_DIM
    ncol = SW_GROUP * w

    @pl.when((b == 0) & (i == 0))
    def _():
        kk = lax.broadcasted_iota(jnp.int32, (2 * w, ncol), 0)
        col = lax.broadcasted_iota(jnp.int32, (2 * w, ncol), 1)
        qq = col & (w - 1)
        grp = col >> 7
        for c in range(SW_KV_HEADS):
            slope = jnp.zeros((2 * w, ncol), F32)
            for pos, g in enumerate(SW_COL_ORDER):
                slope = jnp.where(grp == pos, slope_ref[c * SW_GROUP + g], slope)
            slope = slope * LOG2E
            dist = qq + w - kk
            tab_ref[c, 0] = jnp.where((dist >= 0) & (dist < w), -slope * dist.astype(F32), NEG_INF)
            dist0 = qq - kk
            tab_ref[c, 1] = jnp.where(dist0 >= 0, -slope * dist0.astype(F32), NEG_INF)

    lane = lax.broadcasted_iota(jnp.int32, (w, LANES), 1)
    n_blk = TQ_SW // w
    pad_row = lax.broadcasted_iota(jnp.int32, (DA_VT_PAD, 2 * w), 0)
    ones_rows = jnp.where(pad_row == 0, 1.0, 0.0).astype(BF16)

    def window(bb):
        n = i * n_blk + bb
        if bb == 0:
            return pl.multiple_of(jnp.maximum(n - 1, 0) * w, w), jnp.where(n == 0, 1, 0)
        return pl.multiple_of((n - 1) * w, w), 0

    def scores(bb):
        start, tsel = window(bb)
        kwin = k_ref[pl.ds(start, 2 * w), :]
        kswp = jnp.concatenate([kwin[:, d:], kwin[:, :d]], axis=1)
        qblk = q_ref[bb * w:(bb + 1) * w, :]
        out = []
        for c in range(SW_KV_HEADS):
            s_parts = []
            for pair in (SW_COL_ORDER[:2], SW_COL_ORDER[2:]):
                qz = []
                for g in pair:
                    hh = c * SW_GROUP + g
                    qpair = qblk[:, (hh // 2) * LANES:(hh // 2 + 1) * LANES]
                    keep = (lane >= d) if hh % 2 else (lane < d)
                    qz.append(jnp.where(keep, qpair, jnp.zeros_like(qpair)))
                kk_ = kwin if pair[0] % 2 == c else kswp
                s_parts.append(lax.dot_general(kk_, jnp.concatenate(qz, axis=0),
                                               (((1,), (1,)), ((), ())),
                                               preferred_element_type=F32))
            out.append(jnp.concatenate(s_parts, axis=1) + tab_ref[c, tsel])
        return out

    def softmax_pv(bb, u_all):
        start, _ = window(bb)
        v_t = v_ref[pl.ds(start, 2 * w), :].astype(F32).T.astype(BF16)
        o_rows = []
        for c in range(SW_KV_HEADS):
            u = u_all[c]
            sink = sink_ref[c] * LOG2E
            m = jnp.maximum(jnp.max(u, axis=0, keepdims=True), sink)
            p = jnp.exp2(u - m).astype(BF16)
            v_ext = jnp.concatenate([v_t[c * d:(c + 1) * d, :], ones_rows], axis=0)
            pv = jnp.dot(v_ext, p, preferred_element_type=F32)
            inv = 1.0 / (pv[d:d + 1, :] + jnp.exp2(sink - m))
            o_c = pv[:d, :] * inv
            for g in range(SW_GROUP):
                col = SW_COL_ORDER.index(g) * w
                o_rows.append(o_c[:, col:col + w])
        o_t = jnp.concatenate(o_rows, axis=0)
        o_ref[bb * w:(bb + 1) * w, :] = o_t.T.astype(BF16)

    u_next = scores(0)
    for bb in range(n_blk):
        u_cur = u_next
        if bb + 1 < n_blk:
            u_next = scores(bb + 1)
        softmax_pv(bb, u_cur)


def _sw_attention(attn, slopes, sink_rows):
    nq = SEQ // TQ_SW
    width = SW_HEADS * SW_HEAD_DIM
    return pl.pallas_call(
        _sw_kernel,
        grid=(BATCH, nq),
        in_specs=[
            pl.BlockSpec(memory_space=pltpu.SMEM),
            pl.BlockSpec((SW_KV_HEADS, 1, SW_GROUP * SW_WINDOW), lambda b, i: (0, 0, 0)),
            pl.BlockSpec((TQ_SW, width), lambda b, i: (b * nq + i, SW_Q_BLK512)),
            pl.BlockSpec((SEQ, LANES), lambda b, i: (b, SW_K_BLK)),
            pl.BlockSpec((SEQ, LANES), lambda b, i: (b, SW_V_BLK)),
        ],
        out_specs=pl.BlockSpec((TQ_SW, width), lambda b, i: (b * nq + i, 0)),
        out_shape=jax.ShapeDtypeStruct((ROWS, width), BF16),
        scratch_shapes=[
            pltpu.VMEM((SW_KV_HEADS, 2, 2 * SW_WINDOW, SW_GROUP * SW_WINDOW), F32),
        ],
        compiler_params=pltpu.CompilerParams(
            dimension_semantics=("arbitrary", "arbitrary"), vmem_limit_bytes=VMEM_LIMIT),
        name="sw_attn",
    )(slopes, sink_rows, attn, attn, attn)


def _sigmoid(x):
    return 1.0 / (1.0 + jnp.exp(-x))


def _tail_kernel(yda_ref, ysw_ref, g_ref, x_ref, wda_ref, wsw_ref, wmix_ref, nw_ref,
                 wup_ref, cw_ref, cb_ref, wdn_ref, fw_ref, o_ref, carry_ref, *, final):
    i = pl.program_id(0)
    tm = TM_TAIL
    seq_start = (i % (SEQ // tm)) == 0
    sub = lax.broadcasted_iota(jnp.int32, (SUBLANES, FF_CH), 0)

    t_da = jnp.dot(yda_ref[...], wda_ref[...], preferred_element_type=F32)
    t_sw = jnp.dot(ysw_ref[...], wsw_ref[...], preferred_element_type=F32)
    g_da = g_ref[:, :D_MODEL].astype(F32)
    g_sw = g_ref[:, D_MODEL:].astype(F32)
    merged = _sigmoid(g_da) * t_da + _sigmoid(g_sw) * t_sw
    x_mid = x_ref[...] + jnp.dot(merged.astype(BF16), wmix_ref[...], preferred_element_type=F32)
    o_ref[...] = x_mid
    xn = _rms(x_mid, nw_ref[...]).astype(BF16)

    def shifted(u, prev, k):
        rolled = pltpu.roll(u, k, 0)
        head = jnp.where(sub < k, pltpu.roll(prev, k, 0), rolled[:SUBLANES])
        return jnp.concatenate([head, rolled[SUBLANES:]], axis=0)

    def conv_chunk(col0):
        sl = slice(col0, col0 + FF_CH)
        u = jnp.dot(xn, wup_ref[:, sl], preferred_element_type=F32)
        prev = jnp.where(seq_start, 0.0, carry_ref[:, sl])
        carry_ref[:, sl] = u[tm - SUBLANES:, :]
        return (cb_ref[:, sl]
                + shifted(u, prev, 2) * cw_ref[0:1, sl]
                + shifted(u, prev, 1) * cw_ref[1:2, sl]
                + u * cw_ref[2:3, sl])

    h = []
    for c in range(N_FF_CH):
        gate = conv_chunk(c * FF_CH)
        val = conv_chunk(D_FF + c * FF_CH)
        h.append(((gate * _sigmoid(gate)) * val).astype(BF16))
    h = jnp.concatenate(h, axis=1)

    out = o_ref[...] + jnp.dot(h, wdn_ref[...], preferred_element_type=F32)
    if final:
        out = _rms(out, fw_ref[...])
    o_ref[...] = out


def _layer_tail(yda, ysw, gates, x, wda, wsw, wmix, nw, wup, cw, cb, wdn, fw, layer):
    half = DA_HEADS * 2 * DA_HEAD_DIM
    row = lambda i: (i, 0)
    fixed = lambda i: (0, 0)
    of_layer = lambda i: (layer, 0, 0)
    resident = dict(pipeline_mode=pl.Buffered(1))
    return pl.pallas_call(
        functools.partial(_tail_kernel, final=layer == DEPTH - 1),
        grid=(ROWS // TM_TAIL,),
        in_specs=[
            pl.BlockSpec((TM_TAIL, half), row),
            pl.BlockSpec((TM_TAIL, half), row),
            pl.BlockSpec((TM_TAIL, GATE_COLS), row),
            pl.BlockSpec((TM_TAIL, D_MODEL), row),
            pl.BlockSpec((None, half, D_MODEL), of_layer, **resident),
            pl.BlockSpec((None, half, D_MODEL), of_layer, **resident),
            pl.BlockSpec((None, D_MODEL, D_MODEL), of_layer, **resident),
            pl.BlockSpec((1, D_MODEL), fixed),
            pl.BlockSpec((None, D_MODEL, 2 * D_FF), of_layer, **resident),
            pl.BlockSpec((CONV_WIDTH, 2 * D_FF), fixed),
            pl.BlockSpec((1, 2 * D_FF), fixed),
            pl.BlockSpec((None, D_FF, D_MODEL), of_layer, **resident),
            pl.BlockSpec((1, D_MODEL), fixed),
        ],
        out_specs=pl.BlockSpec((TM_TAIL, D_MODEL), row),
        out_shape=jax.ShapeDtypeStruct((ROWS, D_MODEL), F32),
        scratch_shapes=[
            pltpu.VMEM((SUBLANES, 2 * D_FF), F32),
        ],
        compiler_params=pltpu.CompilerParams(
            dimension_semantics=("arbitrary",), vmem_limit_bytes=VMEM_LIMIT),
        name="layer_tail",
    )(yda, ysw, gates, x, wda, wsw, wmix, nw, wup, cw, cb, wdn, fw)


def _alibi_slopes(n_heads):
    hh = jnp.arange(1, n_heads + 1, dtype=F32)
    return jnp.exp2(-8.0 * hh / n_heads)


def kernel(x, norm_mix_w, w_in, lambda_q1, lambda_k1, lambda_q2, lambda_k2, subln_w, sinks,
           w_br_da, w_br_sw, w_mix_out, norm_ffn_w, w_up, conv_w, conv_b, w_down, norm_final_w):
    xf = x.reshape(ROWS, D_MODEL).astype(F32)
    w_in_b = w_in.astype(BF16)
    w_da_b = w_br_da.astype(BF16)
    w_sw_b = w_br_sw.astype(BF16)
    w_mix_b = w_mix_out.astype(BF16)
    w_up_b = w_up.astype(BF16)
    w_dn_b = w_down.astype(BF16)
    da_slopes = _alibi_slopes(DA_HEADS)
    sw_slopes = _alibi_slopes(SW_HEADS)

    for l in range(DEPTH):
        lam_init = 0.8 - 0.6 * math.exp(-0.3 * l)
        attn, gates = _inproj(xf, norm_mix_w[l].reshape(1, D_MODEL).astype(F32), w_in_b, l)
        da_scal = jnp.concatenate([da_slopes, jnp.full((1,), lam_init, F32)])
        vec = lambda a: a[l].reshape(1, DA_HEAD_DIM).astype(F32)
        y_da = _da_attention(attn, da_scal, vec(lambda_q1), vec(lambda_k1), vec(lambda_q2),
                             vec(lambda_k2), subln_w[l].reshape(2 * DA_HEAD_DIM, 1).astype(F32))
        sink_cols = sinks[l].astype(F32).reshape(SW_KV_HEADS, SW_GROUP)[:, list(SW_COL_ORDER)]
        sink_rows = jnp.repeat(sink_cols, SW_WINDOW, axis=1).reshape(
            SW_KV_HEADS, 1, SW_GROUP * SW_WINDOW)
        y_sw = _sw_attention(attn, sw_slopes, sink_rows)
        xf = _layer_tail(y_da, y_sw, gates, xf, w_da_b, w_sw_b, w_mix_b,
                         norm_ffn_w[l].reshape(1, D_MODEL).astype(F32), w_up_b,
                         conv_w[l].astype(F32), conv_b[l].reshape(1, 2 * D_FF).astype(F32),
                         w_dn_b, norm_final_w.reshape(1, D_MODEL).astype(F32), l)
    return xf.reshape(BATCH, SEQ, D_MODEL)
```

```python
import functools
import math

import jax
import jax.numpy as jnp
import numpy as np
from jax import lax
from jax.experimental import pallas as pl
from jax.experimental.pallas import tpu as pltpu

D_MODEL = 1024
BATCH = 4
SEQ = 4096
DEPTH = 4
ROWS = BATCH * SEQ

DA_HEADS = 4
DA_HEAD_DIM = 64
SW_HEADS = 8
SW_KV_HEADS = 2
SW_HEAD_DIM = 64
SW_GROUP = SW_HEADS // SW_KV_HEADS
SW_WINDOW = 128
D_FF = 2816
CONV_WIDTH = 3
EPS = 1e-6
NEG_INF = -1e30

ATTN_COLS = 2304
GATE_COLS = 2 * D_MODEL
IN_COLS = ATTN_COLS + GATE_COLS
LANES = 128
SUBLANES = 8
DA_K_BLK = 4
DA_V_BLK = 8
SW_Q_BLK512 = 3
SW_K_BLK = 16
SW_V_BLK = 17

VMEM_LIMIT = 56 * 1024 * 1024

BF16 = jnp.bfloat16
F32 = jnp.float32

TM_PROJ = 1024
PROJ_CH = 256
TQ_DA = 1024
DA_TPS = SEQ // TQ_DA
DA_HPS = 1
TK_DA = 512
DA_VT_PAD = 16
DA_NT = 256
DA_POS_SHIFT = 8
TQ_SW = 1024
SW_COL_ORDER = (0, 2, 1, 3)
TM_TAIL = 512
FF_CH = 256
N_FF_CH = D_FF // FF_CH


def _bf16_terms(x, n):
    terms, rest = [], float(x)
    for _ in range(n):
        t = float(np.asarray(rest, dtype=BF16).astype(np.float64))
        terms.append(t)
        rest -= t
    assert rest == 0.0, (x, terms)
    return terms


LOG2E = float(np.float32(1.4426950408889634))
LOG2E_TERMS = _bf16_terms(LOG2E, 3)
assert DA_HEAD_DIM == SW_HEAD_DIM
Q_SCALE = LOG2E * DA_HEAD_DIM ** -0.5
Q_COL_RANGES = ((0, DA_HEADS * 2 * DA_HEAD_DIM),
                (SW_Q_BLK512 * 512, SW_Q_BLK512 * 512 + SW_HEADS * SW_HEAD_DIM))


def _rms(x, w):
    ms = jnp.mean(x * x, axis=-1, keepdims=True)
    return (x * lax.rsqrt(ms + EPS)) * w


def _inproj_kernel(x_ref, nw_ref, w_ref, attn_ref, gate_ref):
    xn = _rms(x_ref[...], nw_ref[...]).astype(BF16)
    for c in range(ATTN_COLS // PROJ_CH):
        sl = slice(c * PROJ_CH, (c + 1) * PROJ_CH)
        res = jnp.dot(xn, w_ref[:, sl], preferred_element_type=F32)
        if any(lo <= c * PROJ_CH and (c + 1) * PROJ_CH <= hi for lo, hi in Q_COL_RANGES):
            res = res * Q_SCALE
        attn_ref[:, sl] = res.astype(BF16)
    for c in range(GATE_COLS // PROJ_CH):
        src = slice(ATTN_COLS + c * PROJ_CH, ATTN_COLS + (c + 1) * PROJ_CH)
        dst = slice(c * PROJ_CH, (c + 1) * PROJ_CH)
        gate_ref[:, dst] = jnp.dot(xn, w_ref[:, src], preferred_element_type=F32).astype(BF16)


def _inproj(x, nw, w, layer):
    return pl.pallas_call(
        _inproj_kernel,
        grid=(ROWS // TM_PROJ,),
        in_specs=[
            pl.BlockSpec((TM_PROJ, D_MODEL), lambda i: (i, 0)),
            pl.BlockSpec((1, D_MODEL), lambda i: (0, 0)),
            pl.BlockSpec((None, D_MODEL, IN_COLS), lambda i: (layer, 0, 0),
                         pipeline_mode=pl.Buffered(1)),
        ],
        out_specs=[
            pl.BlockSpec((TM_PROJ, ATTN_COLS), lambda i: (i, 0)),
            pl.BlockSpec((TM_PROJ, GATE_COLS), lambda i: (i, 0)),
        ],
        out_shape=[
            jax.ShapeDtypeStruct((ROWS, ATTN_COLS), BF16),
            jax.ShapeDtypeStruct((ROWS, GATE_COLS), BF16),
        ],
        compiler_params=pltpu.CompilerParams(
            dimension_semantics=("arbitrary",), vmem_limit_bytes=VMEM_LIMIT),
        name="inproj",
    )(x, nw, w)


def _da_kernel(scal_ref, lq1_ref, lk1_ref, lq2_ref, lk2_ref, subw_ref, q_ref, k_ref, v_ref,
               o_ref, pos_ref, mask_ref, vt_ref, qs_ref, sa_ref, sb_ref, sa_max_ref, sb_max_ref,
               acc_ref, m_ref, l_ref):
    b = pl.program_id(0)
    hp = pl.program_id(1)
    tq, tk, nt = TQ_DA, TK_DA, DA_NT
    tiles_per_map = tq // nt
    nt_dims = (((1,), (1,)), ((), ()))
    n_terms = len(LOG2E_TERMS)
    head_lanes = lambda hh: slice(hh * LANES, (hh + 1) * LANES)

    def build_value_rows(hh):
        sub = lax.broadcasted_iota(jnp.int32, (DA_VT_PAD, tk), 0)
        ones_rows = jnp.where(sub == 0, 1.0, 0.0).astype(BF16)
        for jb in range(SEQ // tk):
            v_blk = v_ref[jb * tk:(jb + 1) * tk, head_lanes(hh)].astype(F32)
            vt_ref[hh, jb, :2 * DA_HEAD_DIM, :] = v_blk.T.astype(BF16)
            vt_ref[hh, jb, 2 * DA_HEAD_DIM:, :] = ones_rows

    @pl.when((b == 0) & (hp == 0))
    def _():
        row = lax.broadcasted_iota(jnp.int32, (SEQ, LANES), 0)
        lane = lax.broadcasted_iota(jnp.int32, (SEQ, LANES), 1)
        hi = (row >> DA_POS_SHIFT).astype(F32)
        lo = (row & ((1 << DA_POS_SHIFT) - 1)).astype(F32)
        pos = jnp.where(lane < 2 * n_terms, jnp.where((lane & 1) == 0, hi, lo), 0.0)
        pos_ref[...] = pos.astype(BF16)
        kk = lax.broadcasted_iota(jnp.int32, (nt, nt), 0)
        qq = lax.broadcasted_iota(jnp.int32, (nt, nt), 1)
        mask_ref[...] = jnp.where(kk <= qq, 0.0, NEG_INF)

    lane = lax.broadcasted_iota(jnp.int32, (tq, LANES), 1)

    def build_queries(n):
        hh, k = divmod(n, DA_TPS)
        slope = jnp.full((1, 1), scal_ref[hp * DA_HPS + hh], F32)
        slope_blk = jnp.zeros((tq, LANES), F32)
        for t, term in enumerate(LOG2E_TERMS):
            slope_blk = jnp.where(lane == 2 * t, slope * (term * (1 << DA_POS_SHIFT)), slope_blk)
            slope_blk = jnp.where(lane == 2 * t + 1, slope * term, slope_blk)
        slope_blk = slope_blk.astype(BF16)
        q = q_ref[k * tq:(k + 1) * tq, head_lanes(hh)]
        zero = jnp.zeros_like(q)
        qs_ref[n, :tq, :LANES] = jnp.where(lane < DA_HEAD_DIM, q, zero)
        qs_ref[n, tq:, :LANES] = jnp.where(lane >= DA_HEAD_DIM, q, zero)
        qs_ref[n, :tq, LANES:] = slope_blk
        qs_ref[n, tq:, LANES:] = slope_blk

    def reset_state(n):
        m_ref[n % 2] = jnp.full(m_ref.shape[1:], NEG_INF, F32)
        l_ref[n % 2] = jnp.zeros(l_ref.shape[1:], F32)
        acc_ref[n % 2] = jnp.zeros(acc_ref.shape[1:], F32)

    n_cols = 2 * tiles_per_map
    halves = (range(n_cols // 2), range(n_cols // 2, n_cols))

    subs = tk // nt

    def visible(c, first_sub):
        if first_sub is None:
            return subs, False
        diag = c % tiles_per_map - first_sub
        return max(0, min(subs, diag + 1)), 0 <= diag < subs

    def scores(n, j, buf, tiles=range(n_cols), first_sub=None):
        s_ref, smax_ref = buf
        hh = n // DA_TPS
        keys = pl.ds(pl.multiple_of(j * tk, tk), tk) if isinstance(j, jax.Array) \
            else slice(j * tk, (j + 1) * tk)
        kx = jnp.concatenate([k_ref[keys, head_lanes(hh)], pos_ref[keys, :]], axis=1)
        for c in tiles:
            rows = visible(c, first_sub)[0] * nt
            if rows == 0:
                continue
            cols = slice(c * nt, (c + 1) * nt)
            s = lax.dot_general(kx[:rows], qs_ref[n, cols, :], nt_dims,
                                preferred_element_type=F32)
            s_ref[:rows, cols] = s
            smax_ref[:, cols] = jnp.max(s, axis=0, keepdims=True)

    def softmax_pv(n, j, buf, first_sub, tiles=range(n_cols)):
        s_ref, smax_ref = buf
        hh, st = n // DA_TPS, n % 2
        for c in tiles:
            n_sub, causal = visible(c, first_sub)
            if n_sub == 0:
                continue
            cols = slice(c * nt, (c + 1) * nt)
            rows = n_sub * nt
            s = s_ref[:rows, cols]
            if causal and rows == nt:
                s = s + mask_ref[...]
            elif causal:
                s = jnp.concatenate([s[:rows - nt], s[rows - nt:] + mask_ref[...]], axis=0)
            s_max = smax_ref[:, cols] if first_sub is None else jnp.max(s, axis=0, keepdims=True)
            m_old = m_ref[st, :, cols]
            m_new = jnp.maximum(m_old, s_max)
            alpha = jnp.exp2(m_old - m_new)
            p = jnp.exp2(s - m_new).astype(BF16)
            pv = jnp.dot(vt_ref[hh, j, :, :rows], p, preferred_element_type=F32)
            acc_ref[st, :, cols] = alpha * acc_ref[st, :, cols] + pv[:2 * DA_HEAD_DIM]
            l_ref[st, :, cols] = (alpha * l_ref[st, :, cols]
                                  + pv[2 * DA_HEAD_DIM:2 * DA_HEAD_DIM + 1])
            m_ref[st, :, cols] = m_new

    def overlapped(n_next, j_next, next_ref, n_cur, j_cur, cur_ref, next_sub=None, cur_sub=None,
                   between=None):
        for idx, half in enumerate(halves):
            scores(n_next, j_next, next_ref, half, next_sub)
            if idx == 0 and between is not None:
                between()
            softmax_pv(n_cur, j_cur, cur_ref, cur_sub, half)

    lam_init = jnp.full((1, 1), scal_ref[DA_HEADS], F32)
    lam = (jnp.exp(jnp.sum(lq1_ref[...] * lk1_ref[...], axis=-1, keepdims=True))
           - jnp.exp(jnp.sum(lq2_ref[...] * lk2_ref[...], axis=-1, keepdims=True)) + lam_init)

    def finish(n):
        hh, k = divmod(n, DA_TPS)
        inv = 1.0 / l_ref[n % 2]
        acc = acc_ref[n % 2]
        o = acc[:, :tq] * inv[:, :tq] - lam * (acc[:, tq:] * inv[:, tq:])
        ms = jnp.mean(o * o, axis=0, keepdims=True)
        y = ((o * lax.rsqrt(ms + EPS)) * subw_ref[...]) * (1.0 - lam_init)
        o_ref[k * tq:(k + 1) * tq, head_lanes(hh)] = y.T.astype(BF16)

    assert tq == 2 * tk
    n_tiles = DA_HPS * DA_TPS
    buf_a, buf_b = (sa_ref, sa_max_ref), (sb_ref, sb_max_ref)
    build_queries(0)
    scores(0, 0, buf_a, first_sub=0)
    for hh in range(DA_HPS):
        build_value_rows(hh)
    for n in range(1, n_tiles):
        build_queries(n)
    for n in range(n_tiles):
        k = n % DA_TPS
        reset_state(n)
        finish_prev = functools.partial(finish, n - 1) if n > 0 else None
        if k == 0:
            overlapped(n, 1, buf_b, n, 0, buf_a, next_sub=subs, cur_sub=0, between=finish_prev)
        else:
            overlapped(n, 1, buf_b, n, 0, buf_a, between=finish_prev)

            def pair(t, carry, n=n):
                overlapped(n, 2 * t + 2, buf_a, n, 2 * t + 1, buf_b)
                overlapped(n, 2 * t + 3, buf_b, n, 2 * t + 2, buf_a)
                return carry

            lax.fori_loop(0, k - 1, pair, 0)
            overlapped(n, 2 * k, buf_a, n, 2 * k - 1, buf_b)
            overlapped(n, 2 * k + 1, buf_b, n, 2 * k, buf_a, next_sub=subs, cur_sub=0)
        if n + 1 < n_tiles:
            overlapped(n + 1, 0, buf_a, n, 2 * k + 1, buf_b, cur_sub=subs,
                       next_sub=0 if (n + 1) % DA_TPS == 0 else None)
        else:
            softmax_pv(n, 2 * k + 1, buf_b, subs)
            finish(n)


def _da_attention(attn, scal, lq1, lk1, lq2, lk2, subw):
    vec = pl.BlockSpec((1, DA_HEAD_DIM), lambda b, h: (0, 0))
    width = DA_HPS * LANES
    k_blk, v_blk = DA_K_BLK // DA_HPS, DA_V_BLK // DA_HPS
    return pl.pallas_call(
        _da_kernel,
        grid=(BATCH, DA_HEADS // DA_HPS),
        in_specs=[
            pl.BlockSpec(memory_space=pltpu.SMEM),
            vec, vec, vec, vec,
            pl.BlockSpec((2 * DA_HEAD_DIM, 1), lambda b, h: (0, 0)),
            pl.BlockSpec((SEQ, width), lambda b, h: (b, h)),
            pl.BlockSpec((SEQ, width), lambda b, h: (b, k_blk + h)),
            pl.BlockSpec((SEQ, width), lambda b, h: (b, v_blk + h)),
        ],
        out_specs=pl.BlockSpec((SEQ, width), lambda b, h: (b, h)),
        out_shape=jax.ShapeDtypeStruct((ROWS, DA_HEADS * 2 * DA_HEAD_DIM), BF16),
        scratch_shapes=[
            pltpu.VMEM((SEQ, LANES), BF16),
            pltpu.VMEM((DA_NT, DA_NT), F32),
            pltpu.VMEM((DA_HPS, SEQ // TK_DA, 2 * DA_HEAD_DIM + DA_VT_PAD, TK_DA), BF16),
            pltpu.VMEM((DA_HPS * DA_TPS, 2 * TQ_DA, 2 * LANES), BF16),
            pltpu.VMEM((TK_DA, 2 * TQ_DA), F32),
            pltpu.VMEM((TK_DA, 2 * TQ_DA), F32),
            pltpu.VMEM((1, 2 * TQ_DA), F32),
            pltpu.VMEM((1, 2 * TQ_DA), F32),
            pltpu.VMEM((2, 2 * DA_HEAD_DIM, 2 * TQ_DA), F32),
            pltpu.VMEM((2, 1, 2 * TQ_DA), F32),
            pltpu.VMEM((2, 1, 2 * TQ_DA), F32),
        ],
        compiler_params=pltpu.CompilerParams(
            dimension_semantics=("arbitrary", "arbitrary"),
            vmem_limit_bytes=VMEM_LIMIT),
        name="diff_attn",
    )(scal, lq1, lk1, lq2, lk2, subw, attn, attn, attn)


def _sw_kernel(slope_ref, sink_ref, q_ref, k_ref, v_ref, o_ref, tab_ref):
    b = pl.program_id(0)
    i = pl.program_id(1)
    w = SW_WINDOW
    d = SW_HEAD_DIM
    ncol = SW_GROUP * w

    @pl.when((b == 0) & (i == 0))
    def _():
        kk = lax.broadcasted_iota(jnp.int32, (2 * w, ncol), 0)
        col = lax.broadcasted_iota(jnp.int32, (2 * w, ncol), 1)
        qq = col & (w - 1)
        grp = col >> 7
        for c in range(SW_KV_HEADS):
            slope = jnp.zeros((2 * w, ncol), F32)
            for pos, g in enumerate(SW_COL_ORDER):
                slope = jnp.where(grp == pos, slope_ref[c * SW_GROUP + g], slope)
            slope = slope * LOG2E
            dist = qq + w - kk
            tab_ref[c, 0] = jnp.where((dist >= 0) & (dist < w), -slope * dist.astype(F32), NEG_INF)
            dist0 = qq - kk
            tab_ref[c, 1] = jnp.where(dist0 >= 0, -slope * dist0.astype(F32), NEG_INF)

    lane = lax.broadcasted_iota(jnp.int32, (w, LANES), 1)
    n_blk = TQ_SW // w
    pad_row = lax.broadcasted_iota(jnp.int32, (DA_VT_PAD, 2 * w), 0)
    ones_rows = jnp.where(pad_row == 0, 1.0, 0.0).astype(BF16)

    def window(bb):
        n = i * n_blk + bb
        if bb == 0:
            return pl.multiple_of(jnp.maximum(n - 1, 0) * w, w), jnp.where(n == 0, 1, 0)
        return pl.multiple_of((n - 1) * w, w), 0

    def scores(bb):
        start, tsel = window(bb)
        kwin = k_ref[pl.ds(start, 2 * w), :]
        kswp = jnp.concatenate([kwin[:, d:], kwin[:, :d]], axis=1)
        qblk = q_ref[bb * w:(bb + 1) * w, :]
        out = []
        for c in range(SW_KV_HEADS):
            s_parts = []
            for pair in (SW_COL_ORDER[:2], SW_COL_ORDER[2:]):
                qz = []
                for g in pair:
                    hh = c * SW_GROUP + g
                    qpair = qblk[:, (hh // 2) * LANES:(hh // 2 + 1) * LANES]
                    keep = (lane >= d) if hh % 2 else (lane < d)
                    qz.append(jnp.where(keep, qpair, jnp.zeros_like(qpair)))
                kk_ = kwin if pair[0] % 2 == c else kswp
                s_parts.append(lax.dot_general(kk_, jnp.concatenate(qz, axis=0),
                                               (((1,), (1,)), ((), ())),
                                               preferred_element_type=F32))
            out.append(jnp.concatenate(s_parts, axis=1) + tab_ref[c, tsel])
        return out

    def softmax_pv(bb, u_all):
        start, _ = window(bb)
        v_t = v_ref[pl.ds(start, 2 * w), :].astype(F32).T.astype(BF16)
        o_rows = []
        for c in range(SW_KV_HEADS):
            u = u_all[c]
            sink = sink_ref[c] * LOG2E
            m = jnp.maximum(jnp.max(u, axis=0, keepdims=True), sink)
            p = jnp.exp2(u - m).astype(BF16)
            v_ext = jnp.concatenate([v_t[c * d:(c + 1) * d, :], ones_rows], axis=0)
            pv = jnp.dot(v_ext, p, preferred_element_type=F32)
            inv = 1.0 / (pv[d:d + 1, :] + jnp.exp2(sink - m))
            o_c = pv[:d, :] * inv
            for g in range(SW_GROUP):
                col = SW_COL_ORDER.index(g) * w
                o_rows.append(o_c[:, col:col + w])
        o_t = jnp.concatenate(o_rows, axis=0)
        o_ref[bb * w:(bb + 1) * w, :] = o_t.T.astype(BF16)

    u_next = scores(0)
    for bb in range(n_blk):
        u_cur = u_next
        if bb + 1 < n_blk:
            u_next = scores(bb + 1)
        softmax_pv(bb, u_cur)


def _sw_attention(attn, slopes, sink_rows):
    nq = SEQ // TQ_SW
    width = SW_HEADS * SW_HEAD_DIM
    return pl.pallas_call(
        _sw_kernel,
        grid=(BATCH, nq),
        in_specs=[
            pl.BlockSpec(memory_space=pltpu.SMEM),
            pl.BlockSpec((SW_KV_HEADS, 1, SW_GROUP * SW_WINDOW), lambda b, i: (0, 0, 0)),
            pl.BlockSpec((TQ_SW, width), lambda b, i: (b * nq + i, SW_Q_BLK512)),
            pl.BlockSpec((SEQ, LANES), lambda b, i: (b, SW_K_BLK)),
            pl.BlockSpec((SEQ, LANES), lambda b, i: (b, SW_V_BLK)),
        ],
        out_specs=pl.BlockSpec((TQ_SW, width), lambda b, i: (b * nq + i, 0)),
        out_shape=jax.ShapeDtypeStruct((ROWS, width), BF16),
        scratch_shapes=[
            pltpu.VMEM((SW_KV_HEADS, 2, 2 * SW_WINDOW, SW_GROUP * SW_WINDOW), F32),
        ],
        compiler_params=pltpu.CompilerParams(
            dimension_semantics=("arbitrary", "arbitrary"), vmem_limit_bytes=VMEM_LIMIT),
        name="sw_attn",
    )(slopes, sink_rows, attn, attn, attn)


def _sigmoid(x):
    return 1.0 / (1.0 + jnp.exp(-x))


def _tail_kernel(yda_ref, ysw_ref, g_ref, x_ref, wda_ref, wsw_ref, wmix_ref, nw_ref,
                 wup_ref, cw_ref, cb_ref, wdn_ref, fw_ref, o_ref, carry_ref, *, final):
    i = pl.program_id(0)
    tm = TM_TAIL
    seq_start = (i % (SEQ // tm)) == 0
    sub = lax.broadcasted_iota(jnp.int32, (SUBLANES, FF_CH), 0)

    t_da = jnp.dot(yda_ref[...], wda_ref[...], preferred_element_type=F32)
    t_sw = jnp.dot(ysw_ref[...], wsw_ref[...], preferred_element_type=F32)
    g_da = g_ref[:, :D_MODEL].astype(F32)
    g_sw = g_ref[:, D_MODEL:].astype(F32)
    merged = _sigmoid(g_da) * t_da + _sigmoid(g_sw) * t_sw
    x_mid = x_ref[...] + jnp.dot(merged.astype(BF16), wmix_ref[...], preferred_element_type=F32)
    o_ref[...] = x_mid
    xn = _rms(x_mid, nw_ref[...]).astype(BF16)

    def shifted(u, prev, k):
        rolled = pltpu.roll(u, k, 0)
        head = jnp.where(sub < k, pltpu.roll(prev, k, 0), rolled[:SUBLANES])
        return jnp.concatenate([head, rolled[SUBLANES:]], axis=0)

    def conv_chunk(col0):
        sl = slice(col0, col0 + FF_CH)
        u = jnp.dot(xn, wup_ref[:, sl], preferred_element_type=F32)
        prev = jnp.where(seq_start, 0.0, carry_ref[:, sl])
        carry_ref[:, sl] = u[tm - SUBLANES:, :]
        return (cb_ref[:, sl]
                + shifted(u, prev, 2) * cw_ref[0:1, sl]
                + shifted(u, prev, 1) * cw_ref[1:2, sl]
                + u * cw_ref[2:3, sl])

    h = []
    for c in range(N_FF_CH):
        gate = conv_chunk(c * FF_CH)
        val = conv_chunk(D_FF + c * FF_CH)
        h.append(((gate * _sigmoid(gate)) * val).astype(BF16))
    h = jnp.concatenate(h, axis=1)

    out = o_ref[...] + jnp.dot(h, wdn_ref[...], preferred_element_type=F32)
    if final:
        out = _rms(out, fw_ref[...])
    o_ref[...] = out


def _layer_tail(yda, ysw, gates, x, wda, wsw, wmix, nw, wup, cw, cb, wdn, fw, layer):
    half = DA_HEADS * 2 * DA_HEAD_DIM
    row = lambda i: (i, 0)
    fixed = lambda i: (0, 0)
    of_layer = lambda i: (layer, 0, 0)
    resident = dict(pipeline_mode=pl.Buffered(1))
    return pl.pallas_call(
        functools.partial(_tail_kernel, final=layer == DEPTH - 1),
        grid=(ROWS // TM_TAIL,),
        in_specs=[
            pl.BlockSpec((TM_TAIL, half), row),
            pl.BlockSpec((TM_TAIL, half), row),
            pl.BlockSpec((TM_TAIL, GATE_COLS), row),
            pl.BlockSpec((TM_TAIL, D_MODEL), row),
            pl.BlockSpec((None, half, D_MODEL), of_layer, **resident),
            pl.BlockSpec((None, half, D_MODEL), of_layer, **resident),
            pl.BlockSpec((None, D_MODEL, D_MODEL), of_layer, **resident),
            pl.BlockSpec((1, D_MODEL), fixed),
            pl.BlockSpec((None, D_MODEL, 2 * D_FF), of_layer, **resident),
            pl.BlockSpec((CONV_WIDTH, 2 * D_FF), fixed),
            pl.BlockSpec((1, 2 * D_FF), fixed),
            pl.BlockSpec((None, D_FF, D_MODEL), of_layer, **resident),
            pl.BlockSpec((1, D_MODEL), fixed),
        ],
        out_specs=pl.BlockSpec((TM_TAIL, D_MODEL), row),
        out_shape=jax.ShapeDtypeStruct((ROWS, D_MODEL), F32),
        scratch_shapes=[
            pltpu.VMEM((SUBLANES, 2 * D_FF), F32),
        ],
        compiler_params=pltpu.CompilerParams(
            dimension_semantics=("arbitrary",), vmem_limit_bytes=VMEM_LIMIT),
        name="layer_tail",
    )(yda, ysw, gates, x, wda, wsw, wmix, nw, wup, cw, cb, wdn, fw)


def _alibi_slopes(n_heads):
    hh = jnp.arange(1, n_heads + 1, dtype=F32)
    return jnp.exp2(-8.0 * hh / n_heads)


def kernel(x, norm_mix_w, w_in, lambda_q1, lambda_k1, lambda_q2, lambda_k2, subln_w, sinks,
           w_br_da, w_br_sw, w_mix_out, norm_ffn_w, w_up, conv_w, conv_b, w_down, norm_final_w):
    xf = x.reshape(ROWS, D_MODEL).astype(F32)
    w_in_b = w_in.astype(BF16)
    w_da_b = w_br_da.astype(BF16)
    w_sw_b = w_br_sw.astype(BF16)
    w_mix_b = w_mix_out.astype(BF16)
    w_up_b = w_up.astype(BF16)
    w_dn_b = w_down.astype(BF16)
    da_slopes = _alibi_slopes(DA_HEADS)
    sw_slopes = _alibi_slopes(SW_HEADS)

    for l in range(DEPTH):
        lam_init = 0.8 - 0.6 * math.exp(-0.3 * l)
        attn, gates = _inproj(xf, norm_mix_w[l].reshape(1, D_MODEL).astype(F32), w_in_b, l)
        da_scal = jnp.concatenate([da_slopes, jnp.full((1,), lam_init, F32)])
        vec = lambda a: a[l].reshape(1, DA_HEAD_DIM).astype(F32)
        y_da = _da_attention(attn, da_scal, vec(lambda_q1), vec(lambda_k1), vec(lambda_q2),
                             vec(lambda_k2), subln_w[l].reshape(2 * DA_HEAD_DIM, 1).astype(F32))
        sink_cols = sinks[l].astype(F32).reshape(SW_KV_HEADS, SW_GROUP)[:, list(SW_COL_ORDER)]
        sink_rows = jnp.repeat(sink_cols, SW_WINDOW, axis=1).reshape(
            SW_KV_HEADS, 1, SW_GROUP * SW_WINDOW)
        y_sw = _sw_attention(attn, sw_slopes, sink_rows)
        xf = _layer_tail(y_da, y_sw, gates, xf, w_da_b, w_sw_b, w_mix_b,
                         norm_ffn_w[l].reshape(1, D_MODEL).astype(F32), w_up_b,
                         conv_w[l].astype(F32), conv_b[l].reshape(1, 2 * D_FF).astype(F32),
                         w_dn_b, norm_final_w.reshape(1, D_MODEL).astype(F32), l)
    return xf.reshape(BATCH, SEQ, D_MODEL)
```

```python
import functools
import math

import jax
import jax.numpy as jnp
import numpy as np
from jax import lax
from jax.experimental import pallas as pl
from jax.experimental.pallas import tpu as pltpu

D_MODEL = 1024
BATCH = 4
SEQ = 4096
DEPTH = 4
ROWS = BATCH * SEQ

DA_HEADS = 4
DA_HEAD_DIM = 64
SW_HEADS = 8
SW_KV_HEADS = 2
SW_HEAD_DIM = 64
SW_GROUP = SW_HEADS // SW_KV_HEADS
SW_WINDOW = 128
D_FF = 2816
CONV_WIDTH = 3
EPS = 1e-6
NEG_INF = -1e30

ATTN_COLS = 2304
GATE_COLS = 2 * D_MODEL
IN_COLS = ATTN_COLS + GATE_COLS
LANES = 128
SUBLANES = 8
DA_K_BLK = 4
DA_V_BLK = 8
SW_Q_BLK512 = 3
SW_K_BLK = 16
SW_V_BLK = 17

VMEM_LIMIT = 56 * 1024 * 1024

BF16 = jnp.bfloat16
F32 = jnp.float32

TM_PROJ = 1024
PROJ_CH = 256
TQ_DA = 1024
DA_TPS = SEQ // TQ_DA
DA_HPS = 1
TK_DA = 512
DA_VT_PAD = 16
DA_NT = 256
DA_POS_SHIFT = 8
TQ_SW = 1024
SW_COL_ORDER = (0, 2, 1, 3)
TM_TAIL = 512
FF_CH = 256
N_FF_CH = D_FF // FF_CH


def _bf16_terms(x, n):
    terms, rest = [], float(x)
    for _ in range(n):
        t = float(np.asarray(rest, dtype=BF16).astype(np.float64))
        terms.append(t)
        rest -= t
    assert rest == 0.0, (x, terms)
    return terms


LOG2E = float(np.float32(1.4426950408889634))
LOG2E_TERMS = _bf16_terms(LOG2E, 3)
assert DA_HEAD_DIM == SW_HEAD_DIM
Q_SCALE = LOG2E * DA_HEAD_DIM ** -0.5
Q_COL_RANGES = ((0, DA_HEADS * 2 * DA_HEAD_DIM),
                (SW_Q_BLK512 * 512, SW_Q_BLK512 * 512 + SW_HEADS * SW_HEAD_DIM))


def _rms(x, w):
    ms = jnp.mean(x * x, axis=-1, keepdims=True)
    return (x * lax.rsqrt(ms + EPS)) * w


def _inproj_kernel(x_ref, nw_ref, w_ref, attn_ref, gate_ref):
    xn = _rms(x_ref[...], nw_ref[...]).astype(BF16)
    for c in range(ATTN_COLS // PROJ_CH):
        sl = slice(c * PROJ_CH, (c + 1) * PROJ_CH)
        res = jnp.dot(xn, w_ref[:, sl], preferred_element_type=F32)
        if any(lo <= c * PROJ_CH and (c + 1) * PROJ_CH <= hi for lo, hi in Q_COL_RANGES):
            res = res * Q_SCALE
        attn_ref[:, sl] = res.astype(BF16)
    for c in range(GATE_COLS // PROJ_CH):
        src = slice(ATTN_COLS + c * PROJ_CH, ATTN_COLS + (c + 1) * PROJ_CH)
        dst = slice(c * PROJ_CH, (c + 1) * PROJ_CH)
        gate_ref[:, dst] = jnp.dot(xn, w_ref[:, src], preferred_element_type=F32).astype(BF16)


def _inproj(x, nw, w, layer):
    return pl.pallas_call(
        _inproj_kernel,
        grid=(ROWS // TM_PROJ,),
        in_specs=[
            pl.BlockSpec((TM_PROJ, D_MODEL), lambda i: (i, 0)),
            pl.BlockSpec((1, D_MODEL), lambda i: (0, 0)),
            pl.BlockSpec((None, D_MODEL, IN_COLS), lambda i: (layer, 0, 0),
                         pipeline_mode=pl.Buffered(1)),
        ],
        out_specs=[
            pl.BlockSpec((TM_PROJ, ATTN_COLS), lambda i: (i, 0)),
            pl.BlockSpec((TM_PROJ, GATE_COLS), lambda i: (i, 0)),
        ],
        out_shape=[
            jax.ShapeDtypeStruct((ROWS, ATTN_COLS), BF16),
            jax.ShapeDtypeStruct((ROWS, GATE_COLS), BF16),
        ],
        compiler_params=pltpu.CompilerParams(
            dimension_semantics=("arbitrary",), vmem_limit_bytes=VMEM_LIMIT),
        name="inproj",
    )(x, nw, w)


def _da_kernel(scal_ref, lq1_ref, lk1_ref, lq2_ref, lk2_ref, subw_ref, q_ref, k_ref, v_ref,
               o_ref, pos_ref, mask_ref, vt_ref, qs_ref, sa_ref, sb_ref, sa_max_ref, sb_max_ref,
               acc_ref, m_ref, l_ref):
    b = pl.program_id(0)
    hp = pl.program_id(1)
    tq, tk, nt = TQ_DA, TK_DA, DA_NT
    tiles_per_map = tq // nt
    nt_dims = (((1,), (1,)), ((), ()))
    n_terms = len(LOG2E_TERMS)
    head_lanes = lambda hh: slice(hh * LANES, (hh + 1) * LANES)

    def build_value_rows(hh):
        sub = lax.broadcasted_iota(jnp.int32, (DA_VT_PAD, tk), 0)
        ones_rows = jnp.where(sub == 0, 1.0, 0.0).astype(BF16)
        for jb in range(SEQ // tk):
            v_blk = v_ref[jb * tk:(jb + 1) * tk, head_lanes(hh)].astype(F32)
            vt_ref[hh, jb, :2 * DA_HEAD_DIM, :] = v_blk.T.astype(BF16)
            vt_ref[hh, jb, 2 * DA_HEAD_DIM:, :] = ones_rows

    @pl.when((b == 0) & (hp == 0))
    def _():
        row = lax.broadcasted_iota(jnp.int32, (SEQ, LANES), 0)
        lane = lax.broadcasted_iota(jnp.int32, (SEQ, LANES), 1)
        hi = (row >> DA_POS_SHIFT).astype(F32)
        lo = (row & ((1 << DA_POS_SHIFT) - 1)).astype(F32)
        pos = jnp.where(lane < 2 * n_terms, jnp.where((lane & 1) == 0, hi, lo), 0.0)
        pos_ref[...] = pos.astype(BF16)
        kk = lax.broadcasted_iota(jnp.int32, (nt, nt), 0)
        qq = lax.broadcasted_iota(jnp.int32, (nt, nt), 1)
        mask_ref[...] = jnp.where(kk <= qq, 0.0, NEG_INF)

    lane = lax.broadcasted_iota(jnp.int32, (tq, LANES), 1)

    def build_queries(n):
        hh, k = divmod(n, DA_TPS)
        slope = jnp.full((1, 1), scal_ref[hp * DA_HPS + hh], F32)
        slope_blk = jnp.zeros((tq, LANES), F32)
        for t, term in enumerate(LOG2E_TERMS):
            slope_blk = jnp.where(lane == 2 * t, slope * (term * (1 << DA_POS_SHIFT)), slope_blk)
            slope_blk = jnp.where(lane == 2 * t + 1, slope * term, slope_blk)
        slope_blk = slope_blk.astype(BF16)
        q = q_ref[k * tq:(k + 1) * tq, head_lanes(hh)]
        zero = jnp.zeros_like(q)
        qs_ref[n, :tq, :LANES] = jnp.where(lane < DA_HEAD_DIM, q, zero)
        qs_ref[n, tq:, :LANES] = jnp.where(lane >= DA_HEAD_DIM, q, zero)
        qs_ref[n, :tq, LANES:] = slope_blk
        qs_ref[n, tq:, LANES:] = slope_blk

    def reset_state(n):
        m_ref[n % 2] = jnp.full(m_ref.shape[1:], NEG_INF, F32)
        l_ref[n % 2] = jnp.zeros(l_ref.shape[1:], F32)
        acc_ref[n % 2] = jnp.zeros(acc_ref.shape[1:], F32)

    n_cols = 2 * tiles_per_map
    halves = (range(n_cols // 2), range(n_cols // 2, n_cols))

    subs = tk // nt

    def visible(c, first_sub):
        if first_sub is None:
            return subs, False
        diag = c % tiles_per_map - first_sub
        return max(0, min(subs, diag + 1)), 0 <= diag < subs

    def scores(n, j, buf, tiles=range(n_cols), first_sub=None):
        s_ref, smax_ref = buf
        hh = n // DA_TPS
        keys = pl.ds(pl.multiple_of(j * tk, tk), tk) if isinstance(j, jax.Array) \
            else slice(j * tk, (j + 1) * tk)
        kx = jnp.concatenate([k_ref[keys, head_lanes(hh)], pos_ref[keys, :]], axis=1)
        for c in tiles:
            rows = visible(c, first_sub)[0] * nt
            if rows == 0:
                continue
            cols = slice(c * nt, (c + 1) * nt)
            s = lax.dot_general(kx[:rows], qs_ref[n, cols, :], nt_dims,
                                preferred_element_type=F32)
            s_ref[:rows, cols] = s
            smax_ref[:, cols] = jnp.max(s, axis=0, keepdims=True)

    def softmax_pv(n, j, buf, first_sub, tiles=range(n_cols)):
        s_ref, smax_ref = buf
        hh, st = n // DA_TPS, n % 2
        for c in tiles:
            n_sub, causal = visible(c, first_sub)
            if n_sub == 0:
                continue
            cols = slice(c * nt, (c + 1) * nt)
            rows = n_sub * nt
            s = s_ref[:rows, cols]
            if causal and rows == nt:
                s = s + mask_ref[...]
            elif causal:
                s = jnp.concatenate([s[:rows - nt], s[rows - nt:] + mask_ref[...]], axis=0)
            s_max = smax_ref[:, cols] if first_sub is None else jnp.max(s, axis=0, keepdims=True)
            m_old = m_ref[st, :, cols]
            m_new = jnp.maximum(m_old, s_max)
            alpha = jnp.exp2(m_old - m_new)
            p = jnp.exp2(s - m_new).astype(BF16)
            pv = jnp.dot(vt_ref[hh, j, :, :rows], p, preferred_element_type=F32)
            acc_ref[st, :, cols] = alpha * acc_ref[st, :, cols] + pv[:2 * DA_HEAD_DIM]
            l_ref[st, :, cols] = (alpha * l_ref[st, :, cols]
                                  + pv[2 * DA_HEAD_DIM:2 * DA_HEAD_DIM + 1])
            m_ref[st, :, cols] = m_new

    def overlapped(n_next, j_next, next_ref, n_cur, j_cur, cur_ref, next_sub=None, cur_sub=None,
                   between=None):
        for idx, half in enumerate(halves):
            scores(n_next, j_next, next_ref, half, next_sub)
            if idx == 0 and between is not None:
                between()
            softmax_pv(n_cur, j_cur, cur_ref, cur_sub, half)

    lam_init = jnp.full((1, 1), scal_ref[DA_HEADS], F32)
    lam = (jnp.exp(jnp.sum(lq1_ref[...] * lk1_ref[...], axis=-1, keepdims=True))
           - jnp.exp(jnp.sum(lq2_ref[...] * lk2_ref[...], axis=-1, keepdims=True)) + lam_init)

    def finish(n):
        hh, k = divmod(n, DA_TPS)
        inv = 1.0 / l_ref[n % 2]
        acc = acc_ref[n % 2]
        o = acc[:, :tq] * inv[:, :tq] - lam * (acc[:, tq:] * inv[:, tq:])
        ms = jnp.mean(o * o, axis=0, keepdims=True)
        y = ((o * lax.rsqrt(ms + EPS)) * subw_ref[...]) * (1.0 - lam_init)
        o_ref[k * tq:(k + 1) * tq, head_lanes(hh)] = y.T.astype(BF16)

    assert tq == 2 * tk
    n_tiles = DA_HPS * DA_TPS
    buf_a, buf_b = (sa_ref, sa_max_ref), (sb_ref, sb_max_ref)
    build_queries(0)
    scores(0, 0, buf_a, first_sub=0)
    for hh in range(DA_HPS):
        build_value_rows(hh)
    for n in range(1, n_tiles):
        build_queries(n)
    for n in range(n_tiles):
        k = n % DA_TPS
        reset_state(n)
        finish_prev = functools.partial(finish, n - 1) if n > 0 else None
        if k == 0:
            overlapped(n, 1, buf_b, n, 0, buf_a, next_sub=subs, cur_sub=0, between=finish_prev)
        else:
            overlapped(n, 1, buf_b, n, 0, buf_a, between=finish_prev)

            def pair(t, carry, n=n):
                overlapped(n, 2 * t + 2, buf_a, n, 2 * t + 1, buf_b)
                overlapped(n, 2 * t + 3, buf_b, n, 2 * t + 2, buf_a)
                return carry

            lax.fori_loop(0, k - 1, pair, 0)
            overlapped(n, 2 * k, buf_a, n, 2 * k - 1, buf_b, next_sub=0)
            overlapped(n, 2 * k + 1, buf_b, n, 2 * k, buf_a, next_sub=subs, cur_sub=0)
        if n + 1 < n_tiles:
            overlapped(n + 1, 0, buf_a, n, 2 * k + 1, buf_b, cur_sub=subs,
                       next_sub=0 if (n + 1) % DA_TPS == 0 else None)
        else:
            softmax_pv(n, 2 * k + 1, buf_b, subs)
            finish(n)


def _da_attention(attn, scal, lq1, lk1, lq2, lk2, subw):
    vec = pl.BlockSpec((1, DA_HEAD_DIM), lambda b, h: (0, 0))
    width = DA_HPS * LANES
    k_blk, v_blk = DA_K_BLK // DA_HPS, DA_V_BLK // DA_HPS
    return pl.pallas_call(
        _da_kernel,
        grid=(BATCH, DA_HEADS // DA_HPS),
        in_specs=[
            pl.BlockSpec(memory_space=pltpu.SMEM),
            vec, vec, vec, vec,
            pl.BlockSpec((2 * DA_HEAD_DIM, 1), lambda b, h: (0, 0)),
            pl.BlockSpec((SEQ, width), lambda b, h: (b, h)),
            pl.BlockSpec((SEQ, width), lambda b, h: (b, k_blk + h)),
            pl.BlockSpec((SEQ, width), lambda b, h: (b, v_blk + h)),
        ],
        out_specs=pl.BlockSpec((SEQ, width), lambda b, h: (b, h)),
        out_shape=jax.ShapeDtypeStruct((ROWS, DA_HEADS * 2 * DA_HEAD_DIM), BF16),
        scratch_shapes=[
            pltpu.VMEM((SEQ, LANES), BF16),
            pltpu.VMEM((DA_NT, DA_NT), F32),
            pltpu.VMEM((DA_HPS, SEQ // TK_DA, 2 * DA_HEAD_DIM + DA_VT_PAD, TK_DA), BF16),
            pltpu.VMEM((DA_HPS * DA_TPS, 2 * TQ_DA, 2 * LANES), BF16),
            pltpu.VMEM((TK_DA, 2 * TQ_DA), F32),
            pltpu.VMEM((TK_DA, 2 * TQ_DA), F32),
            pltpu.VMEM((1, 2 * TQ_DA), F32),
            pltpu.VMEM((1, 2 * TQ_DA), F32),
            pltpu.VMEM((2, 2 * DA_HEAD_DIM, 2 * TQ_DA), F32),
            pltpu.VMEM((2, 1, 2 * TQ_DA), F32),
            pltpu.VMEM((2, 1, 2 * TQ_DA), F32),
        ],
        compiler_params=pltpu.CompilerParams(
            dimension_semantics=("arbitrary", "arbitrary"),
            vmem_limit_bytes=VMEM_LIMIT),
        name="diff_attn",
    )(scal, lq1, lk1, lq2, lk2, subw, attn, attn, attn)


def _sw_kernel(slope_ref, sink_ref, q_ref, k_ref, v_ref, o_ref, tab_ref):
    b = pl.program_id(0)
    i = pl.program_id(1)
    w = SW_WINDOW
    d = SW_HEAD_DIM
    ncol = SW_GROUP * w

    @pl.when((b == 0) & (i == 0))
    def _():
        kk = lax.broadcasted_iota(jnp.int32, (2 * w, ncol), 0)
        col = lax.broadcasted_iota(jnp.int32, (2 * w, ncol), 1)
        qq = col & (w - 1)
        grp = col >> 7
        for c in range(SW_KV_HEADS):
            slope = jnp.zeros((2 * w, ncol), F32)
            for pos, g in enumerate(SW_COL_ORDER):
                slope = jnp.where(grp == pos, slope_ref[c * SW_GROUP + g], slope)
            slope = slope * LOG2E
            dist = qq + w - kk
            tab_ref[c, 0] = jnp.where((dist >= 0) & (dist < w), -slope * dist.astype(F32), NEG_INF)
            dist0 = qq - kk
            tab_ref[c, 1] = jnp.where(dist0 >= 0, -slope * dist0.astype(F32), NEG_INF)

    lane = lax.broadcasted_iota(jnp.int32, (w, LANES), 1)
    n_blk = TQ_SW // w
    pad_row = lax.broadcasted_iota(jnp.int32, (DA_VT_PAD, 2 * w), 0)
    ones_rows = jnp.where(pad_row == 0, 1.0, 0.0).astype(BF16)

    def window(bb):
        n = i * n_blk + bb
        if bb == 0:
            return pl.multiple_of(jnp.maximum(n - 1, 0) * w, w), jnp.where(n == 0, 1, 0)
        return pl.multiple_of((n - 1) * w, w), 0

    def scores(bb):
        start, tsel = window(bb)
        kwin = k_ref[pl.ds(start, 2 * w), :]
        kswp = jnp.concatenate([kwin[:, d:], kwin[:, :d]], axis=1)
        qblk = q_ref[bb * w:(bb + 1) * w, :]
        out = []
        for c in range(SW_KV_HEADS):
            s_parts = []
            for pair in (SW_COL_ORDER[:2], SW_COL_ORDER[2:]):
                qz = []
                for g in pair:
                    hh = c * SW_GROUP + g
                    qpair = qblk[:, (hh // 2) * LANES:(hh // 2 + 1) * LANES]
                    keep = (lane >= d) if hh % 2 else (lane < d)
                    qz.append(jnp.where(keep, qpair, jnp.zeros_like(qpair)))
                kk_ = kwin if pair[0] % 2 == c else kswp
                s_parts.append(lax.dot_general(kk_, jnp.concatenate(qz, axis=0),
                                               (((1,), (1,)), ((), ())),
                                               preferred_element_type=F32))
            out.append(jnp.concatenate(s_parts, axis=1) + tab_ref[c, tsel])
        return out

    def softmax_pv(bb, u_all):
        start, _ = window(bb)
        v_t = v_ref[pl.ds(start, 2 * w), :].astype(F32).T.astype(BF16)
        o_rows = []
        for c in range(SW_KV_HEADS):
            u = u_all[c]
            sink = sink_ref[c] * LOG2E
            m = jnp.maximum(jnp.max(u, axis=0, keepdims=True), sink)
            p = jnp.exp2(u - m).astype(BF16)
            v_ext = jnp.concatenate([v_t[c * d:(c + 1) * d, :], ones_rows], axis=0)
            pv = jnp.dot(v_ext, p, preferred_element_type=F32)
            inv = 1.0 / (pv[d:d + 1, :] + jnp.exp2(sink - m))
            o_c = pv[:d, :] * inv
            for g in range(SW_GROUP):
                col = SW_COL_ORDER.index(g) * w
                o_rows.append(o_c[:, col:col + w])
        o_t = jnp.concatenate(o_rows, axis=0)
        o_ref[bb * w:(bb + 1) * w, :] = o_t.T.astype(BF16)

    u_next = scores(0)
    for bb in range(n_blk):
        u_cur = u_next
        if bb + 1 < n_blk:
            u_next = scores(bb + 1)
        softmax_pv(bb, u_cur)


def _sw_attention(attn, slopes, sink_rows):
    nq = SEQ // TQ_SW
    width = SW_HEADS * SW_HEAD_DIM
    return pl.pallas_call(
        _sw_kernel,
        grid=(BATCH, nq),
        in_specs=[
            pl.BlockSpec(memory_space=pltpu.SMEM),
            pl.BlockSpec((SW_KV_HEADS, 1, SW_GROUP * SW_WINDOW), lambda b, i: (0, 0, 0)),
            pl.BlockSpec((TQ_SW, width), lambda b, i: (b * nq + i, SW_Q_BLK512)),
            pl.BlockSpec((SEQ, LANES), lambda b, i: (b, SW_K_BLK)),
            pl.BlockSpec((SEQ, LANES), lambda b, i: (b, SW_V_BLK)),
        ],
        out_specs=pl.BlockSpec((TQ_SW, width), lambda b, i: (b * nq + i, 0)),
        out_shape=jax.ShapeDtypeStruct((ROWS, width), BF16),
        scratch_shapes=[
            pltpu.VMEM((SW_KV_HEADS, 2, 2 * SW_WINDOW, SW_GROUP * SW_WINDOW), F32),
        ],
        compiler_params=pltpu.CompilerParams(
            dimension_semantics=("arbitrary", "arbitrary"), vmem_limit_bytes=VMEM_LIMIT),
        name="sw_attn",
    )(slopes, sink_rows, attn, attn, attn)


def _sigmoid(x):
    return 1.0 / (1.0 + jnp.exp(-x))


def _tail_kernel(yda_ref, ysw_ref, g_ref, x_ref, wda_ref, wsw_ref, wmix_ref, nw_ref,
                 wup_ref, cw_ref, cb_ref, wdn_ref, fw_ref, o_ref, carry_ref, *, final):
    i = pl.program_id(0)
    tm = TM_TAIL
    seq_start = (i % (SEQ // tm)) == 0
    sub = lax.broadcasted_iota(jnp.int32, (SUBLANES, FF_CH), 0)

    t_da = jnp.dot(yda_ref[...], wda_ref[...], preferred_element_type=F32)
    t_sw = jnp.dot(ysw_ref[...], wsw_ref[...], preferred_element_type=F32)
    g_da = g_ref[:, :D_MODEL].astype(F32)
    g_sw = g_ref[:, D_MODEL:].astype(F32)
    merged = _sigmoid(g_da) * t_da + _sigmoid(g_sw) * t_sw
    x_mid = x_ref[...] + jnp.dot(merged.astype(BF16), wmix_ref[...], preferred_element_type=F32)
    o_ref[...] = x_mid
    xn = _rms(x_mid, nw_ref[...]).astype(BF16)

    def shifted(u, prev, k):
        rolled = pltpu.roll(u, k, 0)
        head = jnp.where(sub < k, pltpu.roll(prev, k, 0), rolled[:SUBLANES])
        return jnp.concatenate([head, rolled[SUBLANES:]], axis=0)

    def conv_chunk(col0):
        sl = slice(col0, col0 + FF_CH)
        u = jnp.dot(xn, wup_ref[:, sl], preferred_element_type=F32)
        prev = jnp.where(seq_start, 0.0, carry_ref[:, sl])
        carry_ref[:, sl] = u[tm - SUBLANES:, :]
        return (cb_ref[:, sl]
                + shifted(u, prev, 2) * cw_ref[0:1, sl]
                + shifted(u, prev, 1) * cw_ref[1:2, sl]
                + u * cw_ref[2:3, sl])

    h = []
    for c in range(N_FF_CH):
        gate = conv_chunk(c * FF_CH)
        val = conv_chunk(D_FF + c * FF_CH)
        h.append(((gate * _sigmoid(gate)) * val).astype(BF16))
    h = jnp.concatenate(h, axis=1)

    out = o_ref[...] + jnp.dot(h, wdn_ref[...], preferred_element_type=F32)
    if final:
        out = _rms(out, fw_ref[...])
    o_ref[...] = out


def _layer_tail(yda, ysw, gates, x, wda, wsw, wmix, nw, wup, cw, cb, wdn, fw, layer):
    half = DA_HEADS * 2 * DA_HEAD_DIM
    row = lambda i: (i, 0)
    fixed = lambda i: (0, 0)
    of_layer = lambda i: (layer, 0, 0)
    resident = dict(pipeline_mode=pl.Buffered(1))
    return pl.pallas_call(
        functools.partial(_tail_kernel, final=layer == DEPTH - 1),
        grid=(ROWS // TM_TAIL,),
        in_specs=[
            pl.BlockSpec((TM_TAIL, half), row),
            pl.BlockSpec((TM_TAIL, half), row),
            pl.BlockSpec((TM_TAIL, GATE_COLS), row),
            pl.BlockSpec((TM_TAIL, D_MODEL), row),
            pl.BlockSpec((None, half, D_MODEL), of_layer, **resident),
            pl.BlockSpec((None, half, D_MODEL), of_layer, **resident),
            pl.BlockSpec((None, D_MODEL, D_MODEL), of_layer, **resident),
            pl.BlockSpec((1, D_MODEL), fixed),
            pl.BlockSpec((None, D_MODEL, 2 * D_FF), of_layer, **resident),
            pl.BlockSpec((CONV_WIDTH, 2 * D_FF), fixed),
            pl.BlockSpec((1, 2 * D_FF), fixed),
            pl.BlockSpec((None, D_FF, D_MODEL), of_layer, **resident),
            pl.BlockSpec((1, D_MODEL), fixed),
        ],
        out_specs=pl.BlockSpec((TM_TAIL, D_MODEL), row),
        out_shape=jax.ShapeDtypeStruct((ROWS, D_MODEL), F32),
        scratch_shapes=[
            pltpu.VMEM((SUBLANES, 2 * D_FF), F32),
        ],
        compiler_params=pltpu.CompilerParams(
            dimension_semantics=("arbitrary",), vmem_limit_bytes=VMEM_LIMIT),
        name="layer_tail",
    )(yda, ysw, gates, x, wda, wsw, wmix, nw, wup, cw, cb, wdn, fw)


def _alibi_slopes(n_heads):
    hh = jnp.arange(1, n_heads + 1, dtype=F32)
    return jnp.exp2(-8.0 * hh / n_heads)


def kernel(x, norm_mix_w, w_in, lambda_q1, lambda_k1, lambda_q2, lambda_k2, subln_w, sinks,
           w_br_da, w_br_sw, w_mix_out, norm_ffn_w, w_up, conv_w, conv_b, w_down, norm_final_w):
    xf = x.reshape(ROWS, D_MODEL).astype(F32)
    w_in_b = w_in.astype(BF16)
    w_da_b = w_br_da.astype(BF16)
    w_sw_b = w_br_sw.astype(BF16)
    w_mix_b = w_mix_out.astype(BF16)
    w_up_b = w_up.astype(BF16)
    w_dn_b = w_down.astype(BF16)
    da_slopes = _alibi_slopes(DA_HEADS)
    sw_slopes = _alibi_slopes(SW_HEADS)

    for l in range(DEPTH):
        lam_init = 0.8 - 0.6 * math.exp(-0.3 * l)
        attn, gates = _inproj(xf, norm_mix_w[l].reshape(1, D_MODEL).astype(F32), w_in_b, l)
        da_scal = jnp.concatenate([da_slopes, jnp.full((1,), lam_init, F32)])
        vec = lambda a: a[l].reshape(1, DA_HEAD_DIM).astype(F32)
        y_da = _da_attention(attn, da_scal, vec(lambda_q1), vec(lambda_k1), vec(lambda_q2),
                             vec(lambda_k2), subln_w[l].reshape(2 * DA_HEAD_DIM, 1).astype(F32))
        sink_cols = sinks[l].astype(F32).reshape(SW_KV_HEADS, SW_GROUP)[:, list(SW_COL_ORDER)]
        sink_rows = jnp.repeat(sink_cols, SW_WINDOW, axis=1).reshape(
            SW_KV_HEADS, 1, SW_GROUP * SW_WINDOW)
        y_sw = _sw_attention(attn, sw_slopes, sink_rows)
        xf = _layer_tail(y_da, y_sw, gates, xf, w_da_b, w_sw_b, w_mix_b,
                         norm_ffn_w[l].reshape(1, D_MODEL).astype(F32), w_up_b,
                         conv_w[l].astype(F32), conv_b[l].reshape(1, 2 * D_FF).astype(F32),
                         w_dn_b, norm_final_w.reshape(1, D_MODEL).astype(F32), l)
    return xf.reshape(BATCH, SEQ, D_MODEL)
```

```python
import functools
import math

import jax
import jax.numpy as jnp
import numpy as np
from jax import lax
from jax.experimental import pallas as pl
from jax.experimental.pallas import tpu as pltpu

D_MODEL = 1024
BATCH = 4
SEQ = 4096
DEPTH = 4
ROWS = BATCH * SEQ

DA_HEADS = 4
DA_HEAD_DIM = 64
SW_HEADS = 8
SW_KV_HEADS = 2
SW_HEAD_DIM = 64
SW_GROUP = SW_HEADS // SW_KV_HEADS
SW_WINDOW = 128
D_FF = 2816
CONV_WIDTH = 3
EPS = 1e-6
NEG_INF = -1e30

ATTN_COLS = 2304
GATE_COLS = 2 * D_MODEL
IN_COLS = ATTN_COLS + GATE_COLS
LANES = 128
SUBLANES = 8
DA_K_BLK = 4
DA_V_BLK = 8
SW_Q_BLK512 = 3
SW_K_BLK = 16
SW_V_BLK = 17

VMEM_LIMIT = 56 * 1024 * 1024

BF16 = jnp.bfloat16
F32 = jnp.float32

TM_PROJ = 1024
PROJ_CH = 256
TQ_DA = 1024
DA_TPS = SEQ // TQ_DA
DA_HPS = 1
TK_DA = 512
DA_VT_PAD = 16
DA_NT = 256
DA_POS_SHIFT = 8
TQ_SW = 1024
SW_COL_ORDER = (0, 2, 1, 3)
TM_TAIL = 512
FF_CH = 256
N_FF_CH = D_FF // FF_CH


def _bf16_terms(x, n):
    terms, rest = [], float(x)
    for _ in range(n):
        t = float(np.asarray(rest, dtype=BF16).astype(np.float64))
        terms.append(t)
        rest -= t
    assert rest == 0.0, (x, terms)
    return terms


LOG2E = float(np.float32(1.4426950408889634))
LOG2E_TERMS = _bf16_terms(LOG2E, 3)
assert DA_HEAD_DIM == SW_HEAD_DIM
Q_SCALE = LOG2E * DA_HEAD_DIM ** -0.5
Q_COL_RANGES = ((0, DA_HEADS * 2 * DA_HEAD_DIM),
                (SW_Q_BLK512 * 512, SW_Q_BLK512 * 512 + SW_HEADS * SW_HEAD_DIM))


def _rms(x, w):
    ms = jnp.mean(x * x, axis=-1, keepdims=True)
    return (x * lax.rsqrt(ms + EPS)) * w


def _inproj_kernel(x_ref, nw_ref, w_ref, attn_ref, gate_ref):
    xn = _rms(x_ref[...], nw_ref[...]).astype(BF16)
    for c in range(ATTN_COLS // PROJ_CH):
        sl = slice(c * PROJ_CH, (c + 1) * PROJ_CH)
        res = jnp.dot(xn, w_ref[:, sl], preferred_element_type=F32)
        if any(lo <= c * PROJ_CH and (c + 1) * PROJ_CH <= hi for lo, hi in Q_COL_RANGES):
            res = res * Q_SCALE
        attn_ref[:, sl] = res.astype(BF16)
    for c in range(GATE_COLS // PROJ_CH):
        src = slice(ATTN_COLS + c * PROJ_CH, ATTN_COLS + (c + 1) * PROJ_CH)
        dst = slice(c * PROJ_CH, (c + 1) * PROJ_CH)
        gate_ref[:, dst] = jnp.dot(xn, w_ref[:, src], preferred_element_type=F32).astype(BF16)


def _inproj(x, nw, w, layer):
    return pl.pallas_call(
        _inproj_kernel,
        grid=(ROWS // TM_PROJ,),
        in_specs=[
            pl.BlockSpec((TM_PROJ, D_MODEL), lambda i: (i, 0)),
            pl.BlockSpec((1, D_MODEL), lambda i: (0, 0)),
            pl.BlockSpec((None, D_MODEL, IN_COLS), lambda i: (layer, 0, 0),
                         pipeline_mode=pl.Buffered(1)),
        ],
        out_specs=[
            pl.BlockSpec((TM_PROJ, ATTN_COLS), lambda i: (i, 0)),
            pl.BlockSpec((TM_PROJ, GATE_COLS), lambda i: (i, 0)),
        ],
        out_shape=[
            jax.ShapeDtypeStruct((ROWS, ATTN_COLS), BF16),
            jax.ShapeDtypeStruct((ROWS, GATE_COLS), BF16),
        ],
        compiler_params=pltpu.CompilerParams(
            dimension_semantics=("arbitrary",), vmem_limit_bytes=VMEM_LIMIT),
        name="inproj",
    )(x, nw, w)


def _da_kernel(scal_ref, lq1_ref, lk1_ref, lq2_ref, lk2_ref, subw_ref, q_ref, k_ref, v_ref,
               o_ref, pos_ref, mask_ref, vt_ref, qs_ref, sa_ref, sb_ref, sa_max_ref, sb_max_ref,
               acc_ref, m_ref, l_ref):
    b = pl.program_id(0)
    hp = pl.program_id(1)
    tq, tk, nt = TQ_DA, TK_DA, DA_NT
    tiles_per_map = tq // nt
    nt_dims = (((1,), (1,)), ((), ()))
    n_terms = len(LOG2E_TERMS)
    head_lanes = lambda hh: slice(hh * LANES, (hh + 1) * LANES)

    def build_value_rows(hh):
        sub = lax.broadcasted_iota(jnp.int32, (DA_VT_PAD, tk), 0)
        ones_rows = jnp.where(sub == 0, 1.0, 0.0).astype(BF16)
        for jb in range(SEQ // tk):
            v_blk = v_ref[jb * tk:(jb + 1) * tk, head_lanes(hh)].astype(F32)
            vt_ref[hh, jb, :2 * DA_HEAD_DIM, :] = v_blk.T.astype(BF16)
            vt_ref[hh, jb, 2 * DA_HEAD_DIM:, :] = ones_rows

    @pl.when((b == 0) & (hp == 0))
    def _():
        row = lax.broadcasted_iota(jnp.int32, (SEQ, LANES), 0)
        lane = lax.broadcasted_iota(jnp.int32, (SEQ, LANES), 1)
        hi = (row >> DA_POS_SHIFT).astype(F32)
        lo = (row & ((1 << DA_POS_SHIFT) - 1)).astype(F32)
        pos = jnp.where(lane < 2 * n_terms, jnp.where((lane & 1) == 0, hi, lo), 0.0)
        pos_ref[...] = pos.astype(BF16)
        kk = lax.broadcasted_iota(jnp.int32, (nt, nt), 0)
        qq = lax.broadcasted_iota(jnp.int32, (nt, nt), 1)
        mask_ref[...] = jnp.where(kk <= qq, 0.0, NEG_INF)

    lane = lax.broadcasted_iota(jnp.int32, (tq, LANES), 1)

    def build_queries(n):
        hh, k = divmod(n, DA_TPS)
        slope = jnp.full((1, 1), scal_ref[hp * DA_HPS + hh], F32)
        slope_blk = jnp.zeros((tq, LANES), F32)
        for t, term in enumerate(LOG2E_TERMS):
            slope_blk = jnp.where(lane == 2 * t, slope * (term * (1 << DA_POS_SHIFT)), slope_blk)
            slope_blk = jnp.where(lane == 2 * t + 1, slope * term, slope_blk)
        slope_blk = slope_blk.astype(BF16)
        q = q_ref[k * tq:(k + 1) * tq, head_lanes(hh)]
        zero = jnp.zeros_like(q)
        qs_ref[n, :tq, :LANES] = jnp.where(lane < DA_HEAD_DIM, q, zero)
        qs_ref[n, tq:, :LANES] = jnp.where(lane >= DA_HEAD_DIM, q, zero)
        qs_ref[n, :tq, LANES:] = slope_blk
        qs_ref[n, tq:, LANES:] = slope_blk

    def reset_state(n):
        m_ref[n % 2] = jnp.full(m_ref.shape[1:], NEG_INF, F32)
        l_ref[n % 2] = jnp.zeros(l_ref.shape[1:], F32)
        acc_ref[n % 2] = jnp.zeros(acc_ref.shape[1:], F32)

    n_cols = 2 * tiles_per_map
    halves = tuple(range(c, c + 2) for c in range(0, n_cols, 2))

    subs = tk // nt

    def visible(c, first_sub):
        if first_sub is None:
            return subs, False
        diag = c % tiles_per_map - first_sub
        return max(0, min(subs, diag + 1)), 0 <= diag < subs

    def scores(n, j, buf, tiles=range(n_cols), first_sub=None):
        s_ref, smax_ref = buf
        hh = n // DA_TPS
        keys = pl.ds(pl.multiple_of(j * tk, tk), tk) if isinstance(j, jax.Array) \
            else slice(j * tk, (j + 1) * tk)
        kx = jnp.concatenate([k_ref[keys, head_lanes(hh)], pos_ref[keys, :]], axis=1)
        for c in tiles:
            rows = visible(c, first_sub)[0] * nt
            if rows == 0:
                continue
            cols = slice(c * nt, (c + 1) * nt)
            s = lax.dot_general(kx[:rows], qs_ref[n, cols, :], nt_dims,
                                preferred_element_type=F32)
            s_ref[:rows, cols] = s
            smax_ref[:, cols] = jnp.max(s, axis=0, keepdims=True)

    def softmax_pv(n, j, buf, first_sub, tiles=range(n_cols)):
        s_ref, smax_ref = buf
        hh, st = n // DA_TPS, n % 2
        for c in tiles:
            n_sub, causal = visible(c, first_sub)
            if n_sub == 0:
                continue
            cols = slice(c * nt, (c + 1) * nt)
            rows = n_sub * nt
            s = s_ref[:rows, cols]
            if causal and rows == nt:
                s = s + mask_ref[...]
            elif causal:
                s = jnp.concatenate([s[:rows - nt], s[rows - nt:] + mask_ref[...]], axis=0)
            s_max = smax_ref[:, cols] if first_sub is None else jnp.max(s, axis=0, keepdims=True)
            m_old = m_ref[st, :, cols]
            m_new = jnp.maximum(m_old, s_max)
            alpha = jnp.exp2(m_old - m_new)
            p = jnp.exp2(s - m_new).astype(BF16)
            pv = jnp.dot(vt_ref[hh, j, :, :rows], p, preferred_element_type=F32)
            acc_ref[st, :, cols] = alpha * acc_ref[st, :, cols] + pv[:2 * DA_HEAD_DIM]
            l_ref[st, :, cols] = (alpha * l_ref[st, :, cols]
                                  + pv[2 * DA_HEAD_DIM:2 * DA_HEAD_DIM + 1])
            m_ref[st, :, cols] = m_new

    def overlapped(n_next, j_next, next_ref, n_cur, j_cur, cur_ref, next_sub=None, cur_sub=None,
                   between=None):
        for idx, half in enumerate(halves):
            scores(n_next, j_next, next_ref, half, next_sub)
            if idx == 0 and between is not None:
                between()
            softmax_pv(n_cur, j_cur, cur_ref, cur_sub, half)

    lam_init = jnp.full((1, 1), scal_ref[DA_HEADS], F32)
    lam = (jnp.exp(jnp.sum(lq1_ref[...] * lk1_ref[...], axis=-1, keepdims=True))
           - jnp.exp(jnp.sum(lq2_ref[...] * lk2_ref[...], axis=-1, keepdims=True)) + lam_init)

    def finish(n):
        hh, k = divmod(n, DA_TPS)
        inv = 1.0 / l_ref[n % 2]
        acc = acc_ref[n % 2]
        o = acc[:, :tq] * inv[:, :tq] - lam * (acc[:, tq:] * inv[:, tq:])
        ms = jnp.mean(o * o, axis=0, keepdims=True)
        y = ((o * lax.rsqrt(ms + EPS)) * subw_ref[...]) * (1.0 - lam_init)
        o_ref[k * tq:(k + 1) * tq, head_lanes(hh)] = y.T.astype(BF16)

    assert tq == 2 * tk
    n_tiles = DA_HPS * DA_TPS
    buf_a, buf_b = (sa_ref, sa_max_ref), (sb_ref, sb_max_ref)
    build_queries(0)
    scores(0, 0, buf_a, first_sub=0)
    for hh in range(DA_HPS):
        build_value_rows(hh)
    for n in range(1, n_tiles):
        build_queries(n)
    for n in range(n_tiles):
        k = n % DA_TPS
        reset_state(n)
        finish_prev = functools.partial(finish, n - 1) if n > 0 else None
        if k == 0:
            overlapped(n, 1, buf_b, n, 0, buf_a, next_sub=subs, cur_sub=0, between=finish_prev)
        else:
            overlapped(n, 1, buf_b, n, 0, buf_a, between=finish_prev)

            def pair(t, carry, n=n):
                overlapped(n, 2 * t + 2, buf_a, n, 2 * t + 1, buf_b)
                overlapped(n, 2 * t + 3, buf_b, n, 2 * t + 2, buf_a)
                return carry

            lax.fori_loop(0, k - 1, pair, 0)
            overlapped(n, 2 * k, buf_a, n, 2 * k - 1, buf_b)
            overlapped(n, 2 * k + 1, buf_b, n, 2 * k, buf_a, next_sub=subs, cur_sub=0)
        if n + 1 < n_tiles:
            overlapped(n + 1, 0, buf_a, n, 2 * k + 1, buf_b, cur_sub=subs,
                       next_sub=0 if (n + 1) % DA_TPS == 0 else None)
        else:
            softmax_pv(n, 2 * k + 1, buf_b, subs)
            finish(n)


def _da_attention(attn, scal, lq1, lk1, lq2, lk2, subw):
    vec = pl.BlockSpec((1, DA_HEAD_DIM), lambda b, h: (0, 0))
    width = DA_HPS * LANES
    k_blk, v_blk = DA_K_BLK // DA_HPS, DA_V_BLK // DA_HPS
    return pl.pallas_call(
        _da_kernel,
        grid=(BATCH, DA_HEADS // DA_HPS),
        in_specs=[
            pl.BlockSpec(memory_space=pltpu.SMEM),
            vec, vec, vec, vec,
            pl.BlockSpec((2 * DA_HEAD_DIM, 1), lambda b, h: (0, 0)),
            pl.BlockSpec((SEQ, width), lambda b, h: (b, h)),
            pl.BlockSpec((SEQ, width), lambda b, h: (b, k_blk + h)),
            pl.BlockSpec((SEQ, width), lambda b, h: (b, v_blk + h)),
        ],
        out_specs=pl.BlockSpec((SEQ, width), lambda b, h: (b, h)),
        out_shape=jax.ShapeDtypeStruct((ROWS, DA_HEADS * 2 * DA_HEAD_DIM), BF16),
        scratch_shapes=[
            pltpu.VMEM((SEQ, LANES), BF16),
            pltpu.VMEM((DA_NT, DA_NT), F32),
            pltpu.VMEM((DA_HPS, SEQ // TK_DA, 2 * DA_HEAD_DIM + DA_VT_PAD, TK_DA), BF16),
            pltpu.VMEM((DA_HPS * DA_TPS, 2 * TQ_DA, 2 * LANES), BF16),
            pltpu.VMEM((TK_DA, 2 * TQ_DA), F32),
            pltpu.VMEM((TK_DA, 2 * TQ_DA), F32),
            pltpu.VMEM((1, 2 * TQ_DA), F32),
            pltpu.VMEM((1, 2 * TQ_DA), F32),
            pltpu.VMEM((2, 2 * DA_HEAD_DIM, 2 * TQ_DA), F32),
            pltpu.VMEM((2, 1, 2 * TQ_DA), F32),
            pltpu.VMEM((2, 1, 2 * TQ_DA), F32),
        ],
        compiler_params=pltpu.CompilerParams(
            dimension_semantics=("arbitrary", "arbitrary"),
            vmem_limit_bytes=VMEM_LIMIT),
        name="diff_attn",
    )(scal, lq1, lk1, lq2, lk2, subw, attn, attn, attn)


def _sw_kernel(slope_ref, sink_ref, q_ref, k_ref, v_ref, o_ref, tab_ref):
    b = pl.program_id(0)
    i = pl.program_id(1)
    w = SW_WINDOW
    d = SW_HEAD_DIM
    ncol = SW_GROUP * w

    @pl.when((b == 0) & (i == 0))
    def _():
        kk = lax.broadcasted_iota(jnp.int32, (2 * w, ncol), 0)
        col = lax.broadcasted_iota(jnp.int32, (2 * w, ncol), 1)
        qq = col & (w - 1)
        grp = col >> 7
        for c in range(SW_KV_HEADS):
            slope = jnp.zeros((2 * w, ncol), F32)
            for pos, g in enumerate(SW_COL_ORDER):
                slope = jnp.where(grp == pos, slope_ref[c * SW_GROUP + g], slope)
            slope = slope * LOG2E
            dist = qq + w - kk
            tab_ref[c, 0] = jnp.where((dist >= 0) & (dist < w), -slope * dist.astype(F32), NEG_INF)
            dist0 = qq - kk
            tab_ref[c, 1] = jnp.where(dist0 >= 0, -slope * dist0.astype(F32), NEG_INF)

    lane = lax.broadcasted_iota(jnp.int32, (w, LANES), 1)
    n_blk = TQ_SW // w
    pad_row = lax.broadcasted_iota(jnp.int32, (DA_VT_PAD, 2 * w), 0)
    ones_rows = jnp.where(pad_row == 0, 1.0, 0.0).astype(BF16)

    def window(bb):
        n = i * n_blk + bb
        if bb == 0:
            return pl.multiple_of(jnp.maximum(n - 1, 0) * w, w), jnp.where(n == 0, 1, 0)
        return pl.multiple_of((n - 1) * w, w), 0

    def scores(bb):
        start, tsel = window(bb)
        kwin = k_ref[pl.ds(start, 2 * w), :]
        kswp = jnp.concatenate([kwin[:, d:], kwin[:, :d]], axis=1)
        qblk = q_ref[bb * w:(bb + 1) * w, :]
        out = []
        for c in range(SW_KV_HEADS):
            s_parts = []
            for pair in (SW_COL_ORDER[:2], SW_COL_ORDER[2:]):
                qz = []
                for g in pair:
                    hh = c * SW_GROUP + g
                    qpair = qblk[:, (hh // 2) * LANES:(hh // 2 + 1) * LANES]
                    keep = (lane >= d) if hh % 2 else (lane < d)
                    qz.append(jnp.where(keep, qpair, jnp.zeros_like(qpair)))
                kk_ = kwin if pair[0] % 2 == c else kswp
                s_parts.append(lax.dot_general(kk_, jnp.concatenate(qz, axis=0),
                                               (((1,), (1,)), ((), ())),
                                               preferred_element_type=F32))
            out.append(jnp.concatenate(s_parts, axis=1) + tab_ref[c, tsel])
        return out

    def softmax_pv(bb, u_all):
        start, _ = window(bb)
        v_t = v_ref[pl.ds(start, 2 * w), :].astype(F32).T.astype(BF16)
        o_rows = []
        for c in range(SW_KV_HEADS):
            u = u_all[c]
            sink = sink_ref[c] * LOG2E
            m = jnp.maximum(jnp.max(u, axis=0, keepdims=True), sink)
            p = jnp.exp2(u - m).astype(BF16)
            v_ext = jnp.concatenate([v_t[c * d:(c + 1) * d, :], ones_rows], axis=0)
            pv = jnp.dot(v_ext, p, preferred_element_type=F32)
            inv = 1.0 / (pv[d:d + 1, :] + jnp.exp2(sink - m))
            o_c = pv[:d, :] * inv
            for g in range(SW_GROUP):
                col = SW_COL_ORDER.index(g) * w
                o_rows.append(o_c[:, col:col + w])
        o_t = jnp.concatenate(o_rows, axis=0)
        o_ref[bb * w:(bb + 1) * w, :] = o_t.T.astype(BF16)

    u_next = scores(0)
    for bb in range(n_blk):
        u_cur = u_next
        if bb + 1 < n_blk:
            u_next = scores(bb + 1)
        softmax_pv(bb, u_cur)


def _sw_attention(attn, slopes, sink_rows):
    nq = SEQ // TQ_SW
    width = SW_HEADS * SW_HEAD_DIM
    return pl.pallas_call(
        _sw_kernel,
        grid=(BATCH, nq),
        in_specs=[
            pl.BlockSpec(memory_space=pltpu.SMEM),
            pl.BlockSpec((SW_KV_HEADS, 1, SW_GROUP * SW_WINDOW), lambda b, i: (0, 0, 0)),
            pl.BlockSpec((TQ_SW, width), lambda b, i: (b * nq + i, SW_Q_BLK512)),
            pl.BlockSpec((SEQ, LANES), lambda b, i: (b, SW_K_BLK)),
            pl.BlockSpec((SEQ, LANES), lambda b, i: (b, SW_V_BLK)),
        ],
        out_specs=pl.BlockSpec((TQ_SW, width), lambda b, i: (b * nq + i, 0)),
        out_shape=jax.ShapeDtypeStruct((ROWS, width), BF16),
        scratch_shapes=[
            pltpu.VMEM((SW_KV_HEADS, 2, 2 * SW_WINDOW, SW_GROUP * SW_WINDOW), F32),
        ],
        compiler_params=pltpu.CompilerParams(
            dimension_semantics=("arbitrary", "arbitrary"), vmem_limit_bytes=VMEM_LIMIT),
        name="sw_attn",
    )(slopes, sink_rows, attn, attn, attn)


def _sigmoid(x):
    return 1.0 / (1.0 + jnp.exp(-x))


def _tail_kernel(yda_ref, ysw_ref, g_ref, x_ref, wda_ref, wsw_ref, wmix_ref, nw_ref,
                 wup_ref, cw_ref, cb_ref, wdn_ref, fw_ref, o_ref, carry_ref, *, final):
    i = pl.program_id(0)
    tm = TM_TAIL
    seq_start = (i % (SEQ // tm)) == 0
    sub = lax.broadcasted_iota(jnp.int32, (SUBLANES, FF_CH), 0)

    t_da = jnp.dot(yda_ref[...], wda_ref[...], preferred_element_type=F32)
    t_sw = jnp.dot(ysw_ref[...], wsw_ref[...], preferred_element_type=F32)
    g_da = g_ref[:, :D_MODEL].astype(F32)
    g_sw = g_ref[:, D_MODEL:].astype(F32)
    merged = _sigmoid(g_da) * t_da + _sigmoid(g_sw) * t_sw
    x_mid = x_ref[...] + jnp.dot(merged.astype(BF16), wmix_ref[...], preferred_element_type=F32)
    o_ref[...] = x_mid
    xn = _rms(x_mid, nw_ref[...]).astype(BF16)

    def shifted(u, prev, k):
        rolled = pltpu.roll(u, k, 0)
        head = jnp.where(sub < k, pltpu.roll(prev, k, 0), rolled[:SUBLANES])
        return jnp.concatenate([head, rolled[SUBLANES:]], axis=0)

    def conv_chunk(col0):
        sl = slice(col0, col0 + FF_CH)
        u = jnp.dot(xn, wup_ref[:, sl], preferred_element_type=F32)
        prev = jnp.where(seq_start, 0.0, carry_ref[:, sl])
        carry_ref[:, sl] = u[tm - SUBLANES:, :]
        return (cb_ref[:, sl]
                + shifted(u, prev, 2) * cw_ref[0:1, sl]
                + shifted(u, prev, 1) * cw_ref[1:2, sl]
                + u * cw_ref[2:3, sl])

    h = []
    for c in range(N_FF_CH):
        gate = conv_chunk(c * FF_CH)
        val = conv_chunk(D_FF + c * FF_CH)
        h.append(((gate * _sigmoid(gate)) * val).astype(BF16))
    h = jnp.concatenate(h, axis=1)

    out = o_ref[...] + jnp.dot(h, wdn_ref[...], preferred_element_type=F32)
    if final:
        out = _rms(out, fw_ref[...])
    o_ref[...] = out


def _layer_tail(yda, ysw, gates, x, wda, wsw, wmix, nw, wup, cw, cb, wdn, fw, layer):
    half = DA_HEADS * 2 * DA_HEAD_DIM
    row = lambda i: (i, 0)
    fixed = lambda i: (0, 0)
    of_layer = lambda i: (layer, 0, 0)
    resident = dict(pipeline_mode=pl.Buffered(1))
    return pl.pallas_call(
        functools.partial(_tail_kernel, final=layer == DEPTH - 1),
        grid=(ROWS // TM_TAIL,),
        in_specs=[
            pl.BlockSpec((TM_TAIL, half), row),
            pl.BlockSpec((TM_TAIL, half), row),
            pl.BlockSpec((TM_TAIL, GATE_COLS), row),
            pl.BlockSpec((TM_TAIL, D_MODEL), row),
            pl.BlockSpec((None, half, D_MODEL), of_layer, **resident),
            pl.BlockSpec((None, half, D_MODEL), of_layer, **resident),
            pl.BlockSpec((None, D_MODEL, D_MODEL), of_layer, **resident),
            pl.BlockSpec((1, D_MODEL), fixed),
            pl.BlockSpec((None, D_MODEL, 2 * D_FF), of_layer, **resident),
            pl.BlockSpec((CONV_WIDTH, 2 * D_FF), fixed),
            pl.BlockSpec((1, 2 * D_FF), fixed),
            pl.BlockSpec((None, D_FF, D_MODEL), of_layer, **resident),
            pl.BlockSpec((1, D_MODEL), fixed),
        ],
        out_specs=pl.BlockSpec((TM_TAIL, D_MODEL), row),
        out_shape=jax.ShapeDtypeStruct((ROWS, D_MODEL), F32),
        scratch_shapes=[
            pltpu.VMEM((SUBLANES, 2 * D_FF), F32),
        ],
        compiler_params=pltpu.CompilerParams(
            dimension_semantics=("arbitrary",), vmem_limit_bytes=VMEM_LIMIT),
        name="layer_tail",
    )(yda, ysw, gates, x, wda, wsw, wmix, nw, wup, cw, cb, wdn, fw)


def _alibi_slopes(n_heads):
    hh = jnp.arange(1, n_heads + 1, dtype=F32)
    return jnp.exp2(-8.0 * hh / n_heads)


def kernel(x, norm_mix_w, w_in, lambda_q1, lambda_k1, lambda_q2, lambda_k2, subln_w, sinks,
           w_br_da, w_br_sw, w_mix_out, norm_ffn_w, w_up, conv_w, conv_b, w_down, norm_final_w):
    xf = x.reshape(ROWS, D_MODEL).astype(F32)
    w_in_b = w_in.astype(BF16)
    w_da_b = w_br_da.astype(BF16)
    w_sw_b = w_br_sw.astype(BF16)
    w_mix_b = w_mix_out.astype(BF16)
    w_up_b = w_up.astype(BF16)
    w_dn_b = w_down.astype(BF16)
    da_slopes = _alibi_slopes(DA_HEADS)
    sw_slopes = _alibi_slopes(SW_HEADS)

    for l in range(DEPTH):
        lam_init = 0.8 - 0.6 * math.exp(-0.3 * l)
        attn, gates = _inproj(xf, norm_mix_w[l].reshape(1, D_MODEL).astype(F32), w_in_b, l)
        da_scal = jnp.concatenate([da_slopes, jnp.full((1,), lam_init, F32)])
        vec = lambda a: a[l].reshape(1, DA_HEAD_DIM).astype(F32)
        y_da = _da_attention(attn, da_scal, vec(lambda_q1), vec(lambda_k1), vec(lambda_q2),
                             vec(lambda_k2), subln_w[l].reshape(2 * DA_HEAD_DIM, 1).astype(F32))
        sink_cols = sinks[l].astype(F32).reshape(SW_KV_HEADS, SW_GROUP)[:, list(SW_COL_ORDER)]
        sink_rows = jnp.repeat(sink_cols, SW_WINDOW, axis=1).reshape(
            SW_KV_HEADS, 1, SW_GROUP * SW_WINDOW)
        y_sw = _sw_attention(attn, sw_slopes, sink_rows)
        xf = _layer_tail(y_da, y_sw, gates, xf, w_da_b, w_sw_b, w_mix_b,
                         norm_ffn_w[l].reshape(1, D_MODEL).astype(F32), w_up_b,
                         conv_w[l].astype(F32), conv_b[l].reshape(1, 2 * D_FF).astype(F32),
                         w_dn_b, norm_final_w.reshape(1, D_MODEL).astype(F32), l)
    return xf.reshape(BATCH, SEQ, D_MODEL)
```
